```python
import jax, jax.numpy as jnp
from jax import lax
import numpy as np

D_MODEL = 1024
BATCH = 16
SEQ = 4096
DEPTH = 4

GRID_W = 64
CTX_LEN = 256
D_MIX = D_MODEL
POOL_WIDTH = D_MIX // 2
POOL_WINDOWS = (2, 4, 8, 16)
POOL_GROUP = POOL_WIDTH // len(POOL_WINDOWS)
GLA_WIDTH = D_MIX - POOL_WIDTH
GLA_HEADS = 4
GLA_DV = GLA_WIDTH // GLA_HEADS
GLA_DK = GLA_DV // 2
GLA_KW = GLA_HEADS * GLA_DK
GATE_RANK = 16
GATE_TAU = 16.0
GLA_CHUNK = 64
D_FF = 2816
N_MOD = 9
DN_ALPHA = (2.0 * DEPTH) ** 0.25
DN_BETA = (8.0 * DEPTH) ** -0.25
LN_EPS = 1e-6
IN_SPLITS = (POOL_WIDTH, GLA_KW, GLA_KW, GLA_WIDTH, GLA_WIDTH, GATE_RANK, GATE_RANK)
IN_COLS = sum(IN_SPLITS)

kernel_name = "hybrid_pool_gla_dit_trunk"


def _layernorm(x):
    x32 = x.astype(jnp.float32)
    mu = jnp.mean(x32, axis=-1, keepdims=True)
    var = jnp.mean(jnp.square(x32 - mu), axis=-1, keepdims=True)
    return ((x32 - mu) * lax.rsqrt(var + LN_EPS)).astype(x.dtype)


def _post_norm(x, y, g, b):
    z = (DN_ALPHA * x + y).astype(jnp.float32)
    mu = jnp.mean(z, axis=-1, keepdims=True)
    var = jnp.mean(jnp.square(z - mu), axis=-1, keepdims=True)
    return ((z - mu) * lax.rsqrt(var + LN_EPS) * g + b).astype(x.dtype)


def _modulate(h, shift, scale):
    return h * (1.0 + scale) + shift


def _ada(cond, w, b):
    m = jax.nn.silu(cond) @ w + b
    m = m.reshape(m.shape[:-1] + (N_MOD, D_MODEL))
    m = jnp.moveaxis(m, -2, 0)
    return m[..., None, :]


def _swiglu(h, w_in, w_out):
    g, u = jnp.split(h @ w_in, 2, axis=-1)
    return (jax.nn.silu(g) * u) @ w_out


def _pos_embed_2d(L, dtype):
    rows = L // GRID_W
    f32 = jnp.float32
    r = jnp.repeat(jnp.arange(rows, dtype=f32), GRID_W)
    col = jnp.tile(jnp.arange(GRID_W, dtype=f32), rows)
    quarter = D_MODEL // 4
    omega = 1.0 / (10000.0 ** (jnp.arange(quarter, dtype=f32) / quarter))

    def enc(p):
        a = p[:, None] * omega
        return jnp.concatenate([jnp.sin(a), jnp.cos(a)], axis=-1)

    return jnp.concatenate([enc(r), enc(col)], axis=-1).astype(dtype)


def _pool_mixer(u, pool_w, pool_scale):
    B, L, _ = u.shape
    u32 = u.astype(jnp.float32)
    cs = jnp.concatenate([jnp.zeros((B, 1, POOL_WIDTH), jnp.float32),
                          jnp.cumsum(u32, axis=1)], axis=1)
    t = jnp.arange(L)
    pooled = []
    for g, w in enumerate(POOL_WINDOWS):
        lo = jnp.maximum(t - w // 2, 0)
        hi = jnp.minimum(t + (w - 1 - w // 2), L - 1)
        csg = cs[:, :, g * POOL_GROUP:(g + 1) * POOL_GROUP]
        s = jnp.take(csg, hi + 1, axis=1) - jnp.take(csg, lo, axis=1)
        pooled.append(s / (hi - lo + 1).astype(jnp.float32)[None, :, None])
    pooled = jnp.stack(pooled, axis=2) - u32.reshape(B, L, len(POOL_WINDOWS), POOL_GROUP)
    y = jnp.einsum('blgc,gcd->blgd', pooled, pool_w.astype(jnp.float32))
    return (y.reshape(B, L, POOL_WIDTH) * pool_scale).astype(u.dtype)


def _project(h, w_in, gate_up_f, gate_bias_f, gate_up_b, gate_bias_b):
    B, L, _ = h.shape
    offs = tuple(int(o) for o in np.cumsum(IN_SPLITS)[:-1])
    u, q, k, v, r, gdf, gdb = jnp.split(h @ w_in, offs, axis=-1)

    def heads(t, d):
        return t.reshape(B, L, GLA_HEADS, d).transpose(0, 2, 1, 3)

    la_f = jax.nn.log_sigmoid((gdf @ gate_up_f + gate_bias_f).astype(jnp.float32)) / GATE_TAU
    la_b = jax.nn.log_sigmoid((gdb @ gate_up_b + gate_bias_b).astype(jnp.float32)) / GATE_TAU
    q = heads(q, GLA_DK) * (GLA_DK ** -0.5)
    return (u, q, heads(k, GLA_DK), heads(v, GLA_DV), r,
            heads(la_f, GLA_DK), heads(la_b, GLA_DK))


def _chunks(t):
    B, H, L, d = t.shape
    return t.reshape(B, H, L // GLA_CHUNK, GLA_CHUNK, d)


def _chunk_state_scan(kc, vc, b, S0):
    b_last = b[..., -1:, :]
    k_end = kc * jnp.exp(b_last - b)
    U = jnp.einsum('bhncd,bhnce->nbhde', k_end, vc)
    decay = jnp.moveaxis(jnp.exp(b_last[..., 0, :]), 2, 0)

    def step(S, inp):
        U_n, d_n = inp
        return d_n[..., None] * S + U_n, S

    S_fin, S_starts = lax.scan(step, S0, (U, decay))
    return S_starts, S_fin


def _gla_direction(q, k, v, log_a, S0):
    B, H, L, _ = v.shape
    f32 = jnp.float32
    qc, kc, vc = (_chunks(t.astype(f32)) for t in (q, k, v))
    b = jnp.cumsum(_chunks(log_a), axis=-2)
    S_starts, S_fin = _chunk_state_scan(kc, vc, b, S0)
    q_in = qc * jnp.exp(b)
    k_in = kc * jnp.exp(-b)
    lower = jnp.tril(jnp.ones((GLA_CHUNK, GLA_CHUNK), dtype=bool))
    scores = jnp.where(lower, jnp.einsum('bhnid,bhnjd->bhnij', q_in, k_in), 0.0)
    o = (jnp.einsum('bhnij,bhnje->bhnie', scores, vc)
         + jnp.einsum('bhnid,nbhde->bhnie', q_in, S_starts))
    return o.reshape(B, H, L, GLA_DV), S_fin


def _gla_final_state(k, v, log_a, S0):
    kc, vc = (_chunks(t.astype(jnp.float32)) for t in (k, v))
    b = jnp.cumsum(_chunks(log_a), axis=-2)
    _, S_fin = _chunk_state_scan(kc, vc, b, S0)
    return S_fin


def _gla_output(o, r, norm_g):
    B, H, L, dv = o.shape
    o = o.transpose(0, 2, 1, 3)
    o = o * lax.rsqrt(jnp.mean(jnp.square(o), axis=-1, keepdims=True) + LN_EPS)
    o = o.reshape(B, L, H * dv) * norm_g
    return (o * jax.nn.silu(r.astype(jnp.float32))).astype(r.dtype)


def _flip(t):
    return jnp.flip(t, axis=2)


def _mixer(h, hc, w_in, pool_w, pool_scale, gate_up_f, gate_bias_f, gate_up_b, gate_bias_b,
           norm_g, w_out, ctx_out):
    B = h.shape[0]
    gates = (gate_up_f, gate_bias_f, gate_up_b, gate_bias_b)
    u, q, k, v, r, la_f, la_b = _project(h, w_in, *gates)
    uc, qc, kc, vc, rc, lac_f, lac_b = _project(hc, w_in, *gates)
    S0 = jnp.zeros((B, GLA_HEADS, GLA_DK, GLA_DV), jnp.float32)
    yc = None
    if ctx_out:
        oc_f, S_f = _gla_direction(qc, kc, vc, lac_f, S0)
        oc_b, S_b = _gla_direction(_flip(qc), _flip(kc), _flip(vc), _flip(lac_b), S0)
        yc = jnp.concatenate([_pool_mixer(uc, pool_w, pool_scale),
                              _gla_output(oc_f + _flip(oc_b), rc, norm_g)], axis=-1) @ w_out
    else:
        S_f = _gla_final_state(kc, vc, lac_f, S0)
        S_b = _gla_final_state(_flip(kc), _flip(vc), _flip(lac_b), S0)
    o_f, _ = _gla_direction(q, k, v, la_f, S_f)
    o_b, _ = _gla_direction(_flip(q), _flip(k), _flip(v), _flip(la_b), S_b)
    y = jnp.concatenate([_pool_mixer(u, pool_w, pool_scale),
                         _gla_output(o_f + _flip(o_b), r, norm_g)], axis=-1) @ w_out
    return y, yc


def setup_inputs(seed: int = 0) -> dict:
    key = jax.random.key(seed)
    ks = jax.random.split(key, 24)
    f32 = jnp.float32

    def nrm(k, shape, s):
        return jax.random.normal(k, shape, f32) * s

    D = D_MODEL
    return {
        'x': nrm(ks[0], (BATCH, SEQ, D), 1.0),
        'c': nrm(ks[1], (BATCH, D), 1.0),
        'ctx': nrm(ks[2], (BATCH, CTX_LEN, D), 1.0),
        'c_ctx': nrm(ks[3], (D,), 1.0),
        'w_ada': nrm(ks[4], (DEPTH, D, N_MOD * D), 0.5 * D ** -0.5),
        'b_ada': nrm(ks[5], (DEPTH, N_MOD * D), 0.02),
        'ln_g': 1.0 + nrm(ks[6], (DEPTH, 3, D), 0.05),
        'ln_b': nrm(ks[7], (DEPTH, 3, D), 0.02),
        'ffa_w_in': nrm(ks[8], (DEPTH, D, 2 * D_FF), D ** -0.5),
        'ffa_w_out': nrm(ks[9], (DEPTH, D_FF, D), D_FF ** -0.5 * DN_BETA),
        'mix_w_in': nrm(ks[10], (DEPTH, D, IN_COLS), D ** -0.5),
        'pool_w': nrm(ks[11], (DEPTH, len(POOL_WINDOWS), POOL_GROUP, POOL_GROUP), POOL_GROUP ** -0.5),
        'pool_scale': 1.0 + nrm(ks[12], (DEPTH, POOL_WIDTH), 0.1),
        'gate_up_f': nrm(ks[13], (DEPTH, GATE_RANK, GLA_KW), GATE_RANK ** -0.5),
        'gate_bias_f': jax.random.uniform(ks[14], (DEPTH, GLA_KW), f32, 0.5, 3.0),
        'gate_up_b': nrm(ks[15], (DEPTH, GATE_RANK, GLA_KW), GATE_RANK ** -0.5),
        'gate_bias_b': jax.random.uniform(ks[16], (DEPTH, GLA_KW), f32, 0.5, 3.0),
        'gla_norm_g': 1.0 + nrm(ks[17], (DEPTH, GLA_WIDTH), 0.05),
        'mix_w_out': nrm(ks[18], (DEPTH, D_MIX, D), D_MIX ** -0.5 * DN_BETA),
        'ffb_w_in': nrm(ks[19], (DEPTH, D, 2 * D_FF), D ** -0.5),
        'ffb_w_out': nrm(ks[20], (DEPTH, D_FF, D), D_FF ** -0.5 * DN_BETA),
    }


def reference(x, c, ctx, c_ctx, w_ada, b_ada, ln_g, ln_b, ffa_w_in, ffa_w_out, mix_w_in,
              pool_w, pool_scale, gate_up_f, gate_bias_f, gate_up_b, gate_bias_b,
              gla_norm_g, mix_w_out, ffb_w_in, ffb_w_out):
    L = x.shape[1]
    h = _layernorm(x + _pos_embed_2d(L, x.dtype))
    hc = _layernorm(ctx)
    for l in range(DEPTH):
        last = l == DEPTH - 1
        mx = _ada(c, w_ada[l], b_ada[l])
        mc = _ada(c_ctx, w_ada[l], b_ada[l])
        h = _post_norm(h, 0.5 * mx[2] * _swiglu(_modulate(h, mx[0], mx[1]), ffa_w_in[l], ffa_w_out[l]),
                       ln_g[l, 0], ln_b[l, 0])
        hc = _post_norm(hc, 0.5 * mc[2] * _swiglu(_modulate(hc, mc[0], mc[1]), ffa_w_in[l], ffa_w_out[l]),
                        ln_g[l, 0], ln_b[l, 0])
        y, yc = _mixer(_modulate(h, mx[3], mx[4]), _modulate(hc, mc[3], mc[4]), mix_w_in[l],
                       pool_w[l], pool_scale[l], gate_up_f[l], gate_bias_f[l], gate_up_b[l],
                       gate_bias_b[l], gla_norm_g[l], mix_w_out[l], not last)
        h = _post_norm(h, mx[5] * y, ln_g[l, 1], ln_b[l, 1])
        h = _post_norm(h, 0.5 * mx[8] * _swiglu(_modulate(h, mx[6], mx[7]), ffb_w_in[l], ffb_w_out[l]),
                       ln_g[l, 2], ln_b[l, 2])
        if not last:
            hc = _post_norm(hc, mc[5] * yc, ln_g[l, 1], ln_b[l, 1])
            hc = _post_norm(hc, 0.5 * mc[8] * _swiglu(_modulate(hc, mc[6], mc[7]), ffb_w_in[l], ffb_w_out[l]),
                            ln_g[l, 2], ln_b[l, 2])
    return h
```

```python
import functools

import jax
import jax.numpy as jnp
from jax import lax
from jax.experimental import pallas as pl
from jax.experimental.pallas import tpu as pltpu

F32 = jnp.float32
BF16 = jnp.bfloat16

D_MODEL = 1024
D_FF = 2816
N_MOD = 9
POOL_WINDOWS = (2, 4, 8, 16)
POOL_GROUP = 128
POOL_WIDTH = POOL_GROUP * len(POOL_WINDOWS)
GLA_HEADS = 4
GLA_DK = 64
GLA_DV = 128
GLA_KW = GLA_HEADS * GLA_DK
GLA_WIDTH = GLA_HEADS * GLA_DV
GATE_RANK = 16
GATE_TAU = 16.0
GLA_CHUNK = 64
GRID_W = 64
LN_EPS = 1e-6
MAIN_COLS = POOL_WIDTH + 2 * GLA_KW + 2 * GLA_WIDTH
GD_PAD = 256

VMEM_LIMIT_BYTES = 56 * 1024 * 1024
FFN_TOKENS = 512
FFN_CHUNK = 256
SEQ_TOKENS = 256
HALO = 16


def _dot(a, b):
    return jnp.dot(a, b, preferred_element_type=F32)


def _dot_nt(a, b):
    return lax.dot_general(a, b, (((1,), (1,)), ((), ())), preferred_element_type=F32)


def _dot_tn(a, b):
    return lax.dot_general(a, b, (((0,), (0,)), ((), ())), preferred_element_type=F32)


def _silu(x):
    return x / (1.0 + jnp.exp(-x))


def _layernorm(z):
    mu = jnp.mean(z, axis=-1, keepdims=True)
    zc = z - mu
    var = jnp.mean(zc * zc, axis=-1, keepdims=True)
    return zc * lax.rsqrt(var + LN_EPS)


def _params(*sem):
    return pltpu.CompilerParams(dimension_semantics=sem, vmem_limit_bytes=VMEM_LIMIT_BYTES)


def _resident(shape):
    nd = len(shape)
    return pl.BlockSpec(shape, lambda *_: (0,) * nd, pipeline_mode=pl.Buffered(1))


def _ada_kernel(c_ref, w_ref, b_ref, o_ref):
    s = _silu(c_ref[...])
    o_ref[0] = _dot(s.astype(BF16), w_ref[0].astype(BF16)) + b_ref[0]


def _ada(cond, w_ada, b_ada):
    depth, d, n = w_ada.shape
    rows = cond.shape[0]
    tn = 1024
    return pl.pallas_call(
        _ada_kernel,
        grid=(depth, n // tn),
        in_specs=[
            pl.BlockSpec((rows, d), lambda l, j: (0, 0)),
            pl.BlockSpec((1, d, tn), lambda l, j: (l, 0, j)),
            pl.BlockSpec((1, 1, tn), lambda l, j: (l, 0, j)),
        ],
        out_specs=pl.BlockSpec((1, rows, tn), lambda l, j: (l, 0, j)),
        out_shape=jax.ShapeDtypeStruct((depth, rows, n), F32),
        compiler_params=_params("arbitrary", "arbitrary"),
        name="ada",
    )(cond, w_ada, b_ada.reshape(depth, 1, n))


def _prenorm_kernel(x_ref, pos_ref, ctx_ref, o_ref, *, n_batch):
    is_ctx = pl.program_id(0) == n_batch
    z = jnp.where(is_ctx, ctx_ref[0], x_ref[0] + pos_ref[...])
    o_ref[0] = _layernorm(z)


def _prenorm(x, pos, ctx_flat):
    B, L, D = x.shape
    tm = FFN_TOKENS
    nt = L // tm
    return pl.pallas_call(
        functools.partial(_prenorm_kernel, n_batch=B),
        grid=(B + 1, nt),
        in_specs=[
            pl.BlockSpec((1, tm, D), lambda b, i: (jnp.minimum(b, B - 1), jnp.where(b == B, nt - 1, i), 0)),
            pl.BlockSpec((tm, D), lambda b, i: (jnp.where(b == B, nt - 1, i), 0)),
            pl.BlockSpec((1, tm, D), lambda b, i: (0, jnp.where(b == B, i, 0), 0)),
        ],
        out_specs=pl.BlockSpec((1, tm, D), lambda b, i: (b, i, 0)),
        out_shape=jax.ShapeDtypeStruct((B + 1, L, D), F32),
        compiler_params=_params("arbitrary", "arbitrary"),
        name="prenorm",
    )(x, pos, ctx_flat)


def _ffn_kernel(h_ref, mod_ref, w_in_ref, w_out_ref, g_ref, b_ref, o_ref, a_ref, *, k0, alpha):
    x = h_ref[0]
    shift = mod_ref[0, k0:k0 + 1, :]
    scale = mod_ref[0, k0 + 1:k0 + 2, :]
    gate = mod_ref[0, k0 + 2:k0 + 3, :]
    xm = (x * (1.0 + scale) + shift).astype(BF16)
    for j in range(D_FF // FFN_CHUNK):
        lo = j * FFN_CHUNK
        g = _dot(xm, w_in_ref[:, lo:lo + FFN_CHUNK])
        u = _dot(xm, w_in_ref[:, D_FF + lo:D_FF + lo + FFN_CHUNK])
        a_ref[:, lo:lo + FFN_CHUNK] = (_silu(g) * u).astype(BF16)
    y = _dot(a_ref[...], w_out_ref[...])
    z = alpha * x + (0.5 * gate) * y
    o_ref[0] = _layernorm(z) * g_ref[...] + b_ref[...]


def _ffn(h, mod, w_in, w_out, ln_g, ln_b, *, k0, alpha, nseg):
    _, L, D = h.shape
    tm = FFN_TOKENS
    return pl.pallas_call(
        functools.partial(_ffn_kernel, k0=k0, alpha=alpha),
        grid=(nseg, L // tm),
        in_specs=[
            pl.BlockSpec((1, tm, D), lambda b, i: (b, i, 0)),
            pl.BlockSpec((1, N_MOD, D), lambda b, i: (b, 0, 0)),
            _resident(w_in.shape),
            _resident(w_out.shape),
            _resident((1, D)),
            _resident((1, D)),
        ],
        out_specs=pl.BlockSpec((1, tm, D), lambda b, i: (b, i, 0)),
        out_shape=jax.ShapeDtypeStruct((nseg, L, D), F32),
        scratch_shapes=[pltpu.VMEM((tm, D_FF), BF16)],
        compiler_params=_params("arbitrary", "arbitrary"),
        name="ffn",
    )(h, mod, w_in, w_out, ln_g.reshape(1, D), ln_b.reshape(1, D))


def _inproj_kernel(h_ref, mod_ref, w_ref, gup_ref, gb_ref, p_ref, la_ref):
    x = h_ref[0]
    shift = mod_ref[0, 3:4, :]
    scale = mod_ref[0, 4:5, :]
    xm = (x * (1.0 + scale) + shift).astype(BF16)
    for j in range(MAIN_COLS // 512):
        p_ref[0, :, j * 512:(j + 1) * 512] = _dot(xm, w_ref[:, j * 512:(j + 1) * 512]).astype(BF16)
    gd = _dot(xm, w_ref[:, MAIN_COLS:MAIN_COLS + GD_PAD])
    z = _dot(gd.astype(BF16), gup_ref[...]) + gb_ref[...]
    la_ref[0] = (jnp.minimum(z, 0.0) - jnp.log1p(jnp.exp(-jnp.abs(z)))) * (1.0 / GATE_TAU)


def _inproj(h, mod, w_all, gup, gbias):
    nseg, L, D = h.shape
    tm = FFN_TOKENS
    return pl.pallas_call(
        _inproj_kernel,
        grid=(nseg, L // tm),
        in_specs=[
            pl.BlockSpec((1, tm, D), lambda b, i: (b, i, 0)),
            pl.BlockSpec((1, N_MOD, D), lambda b, i: (b, 0, 0)),
            _resident(w_all.shape),
            _resident(gup.shape),
            _resident(gbias.shape),
        ],
        out_specs=[
            pl.BlockSpec((1, tm, MAIN_COLS), lambda b, i: (b, i, 0)),
            pl.BlockSpec((1, tm, 2 * GLA_KW), lambda b, i: (b, i, 0)),
        ],
        out_shape=[
            jax.ShapeDtypeStruct((nseg, L, MAIN_COLS), BF16),
            jax.ShapeDtypeStruct((nseg, L, 2 * GLA_KW), F32),
        ],
        compiler_params=_params("arbitrary", "arbitrary"),
        name="inproj",
    )(h, mod, w_all, gup, gbias)


def _gla_block(qk_ref, v_ref, la_ref, s_ref, *, fwd):
    tb = SEQ_TOKENS
    nchunk = tb // GLA_CHUNK
    qk = qk_ref[0].astype(F32)
    q = qk[:, :GLA_KW]
    k = qk[:, GLA_KW:]
    v = v_ref[0]
    la = la_ref[0][:, :GLA_KW] if fwd else la_ref[0][:, GLA_KW:]

    row = lax.broadcasted_iota(jnp.int32, (tb, tb), 0)
    col = lax.broadcasted_iota(jnp.int32, (tb, tb), 1)
    same = (row // GLA_CHUNK) == (col // GLA_CHUNK)
    tri = same & ((col <= row) if fwd else (col >= row))
    tri_m = jnp.where(tri, 1.0, 0.0).astype(BF16)
    same_m = jnp.where(same, 1.0, 0.0).astype(BF16)
    la_hi = la.astype(BF16)
    la_lo = (la - la_hi.astype(F32)).astype(BF16)
    bcs = _dot(tri_m, la_hi) + _dot(tri_m, la_lo)
    tot = _dot(same_m, la_hi) + _dot(same_m, la_lo)
    q_in = q * jnp.exp(bcs) * (GLA_DK ** -0.5)
    k_in = (k * jnp.exp(-bcs)).astype(BF16)
    k_end = (k * jnp.exp(tot - bcs)).astype(BF16)

    lane = lax.broadcasted_iota(jnp.int32, (GLA_CHUNK, 2 * GLA_DK), 1)
    first_head = lane < GLA_DK
    srow = lax.broadcasted_iota(jnp.int32, (2 * GLA_CHUNK, GLA_CHUNK), 0) % GLA_CHUNK
    scol = lax.broadcasted_iota(jnp.int32, (2 * GLA_CHUNK, GLA_CHUNK), 1)
    causal = (scol <= srow) if fwd else (scol >= srow)
    ones = jnp.ones((GLA_CHUNK, GLA_DV), BF16)

    outs = []
    order = range(nchunk) if fwd else range(nchunk - 1, -1, -1)
    for c in order:
        rs = slice(c * GLA_CHUNK, (c + 1) * GLA_CHUNK)
        decay = jnp.exp(_dot_tn(la_hi[rs], ones) + _dot_tn(la_lo[rs], ones))
        o_heads = []
        for p in range(GLA_HEADS // 2):
            ls = slice(p * 2 * GLA_DK, (p + 1) * 2 * GLA_DK)
            qp = q_in[rs, ls]
            q2 = jnp.concatenate([jnp.where(first_head, qp, 0.0), jnp.where(first_head, 0.0, qp)],
                                 axis=0).astype(BF16)
            sc = _dot_nt(q2, k_in[rs, ls])
            sc = jnp.where(causal, sc, 0.0).astype(BF16)
            s_pair = s_ref[p * 2 * GLA_DK:(p + 1) * 2 * GLA_DK, :]
            s_pair_b = s_pair.astype(BF16)
            for hh in range(2):
                h = 2 * p + hh
                hs = slice(hh * GLA_CHUNK, (hh + 1) * GLA_CHUNK)
                vh = v[rs, h * GLA_DV:(h + 1) * GLA_DV]
                o_heads.append(_dot(sc[hs], vh) + _dot(q2[hs], s_pair_b))
            upd = _dot_tn(k_end[rs, ls], v[rs, p * 2 * GLA_DV:(p + 1) * 2 * GLA_DV])
            new = jnp.concatenate([upd[:GLA_DK, :GLA_DV], upd[GLA_DK:, GLA_DV:]], axis=0)
            s_ref[p * 2 * GLA_DK:(p + 1) * 2 * GLA_DK, :] = decay[ls, :] * s_pair + new
        outs.append((c, jnp.concatenate(o_heads, axis=1)))
    return outs


def _gla_kernel(qk_ref, v_ref, r_ref, la_ref, ng_ref, o_ref, of_ref, s_ref, *, nb):
    ph = pl.program_id(1)
    j = pl.program_id(2)

    @pl.when(j == 0)
    def _():
        s_ref[...] = jnp.zeros_like(s_ref)

    @pl.when(ph == 0)
    def _():
        base = pl.multiple_of(j * SEQ_TOKENS, SEQ_TOKENS)
        for c, o in _gla_block(qk_ref, v_ref, la_ref, s_ref, fwd=True):
            of_ref[pl.ds(base + c * GLA_CHUNK, GLA_CHUNK), :] = o

    @pl.when(ph == 1)
    def _():
        jb = jnp.where(j == 0, 0, nb + 1 - j)
        base = pl.multiple_of(jb * SEQ_TOKENS, SEQ_TOKENS)
        for c, o in _gla_block(qk_ref, v_ref, la_ref, s_ref, fwd=False):
            rs = slice(c * GLA_CHUNK, (c + 1) * GLA_CHUNK)
            o = o + of_ref[pl.ds(base + c * GLA_CHUNK, GLA_CHUNK), :]
            normed = []
            for h in range(GLA_HEADS):
                oh = o[:, h * GLA_DV:(h + 1) * GLA_DV]
                normed.append(oh * lax.rsqrt(jnp.mean(oh * oh, axis=-1, keepdims=True) + LN_EPS))
            on = jnp.concatenate(normed, axis=1) * ng_ref[...]
            o_ref[0, rs, :] = (on * _silu(r_ref[0, rs, :].astype(F32))).astype(BF16)


def _gla(p, la, norm_g, *, n_batch):
    nseg, L, _ = p.shape
    tb = SEQ_TOKENS
    nb = L // tb

    def blk(b, ph, j):
        jb = jnp.where(ph == 0, j, jnp.where(j == 0, 0, nb + 1 - j))
        return jnp.where(jb == 0, n_batch, b), jnp.where(jb == 0, b, jb - 1)

    def col(cidx):
        def index_map(b, ph, j):
            seg, t = blk(b, ph, j)
            return seg, t, cidx
        return index_map

    def second_pass_only(cidx):
        def index_map(b, ph, j):
            seg, t = blk(b, ph, j)
            return jnp.where(ph == 0, n_batch, seg), jnp.where(ph == 0, b, t), cidx
        return index_map

    return pl.pallas_call(
        functools.partial(_gla_kernel, nb=nb),
        grid=(n_batch, 2, nb + 1),
        in_specs=[
            pl.BlockSpec((1, tb, 2 * GLA_KW), col(1)),
            pl.BlockSpec((1, tb, GLA_WIDTH), col(2)),
            pl.BlockSpec((1, tb, GLA_WIDTH), second_pass_only(3)),
            pl.BlockSpec((1, tb, 2 * GLA_KW), col(0)),
            _resident((1, GLA_WIDTH)),
        ],
        out_specs=pl.BlockSpec((1, tb, GLA_WIDTH), second_pass_only(0)),
        out_shape=jax.ShapeDtypeStruct((nseg, L, GLA_WIDTH), BF16),
        scratch_shapes=[
            pltpu.VMEM(((nb + 1) * tb, GLA_WIDTH), F32),
            pltpu.VMEM((GLA_HEADS * GLA_DK, GLA_DV), F32),
        ],
        compiler_params=_params("arbitrary", "arbitrary", "arbitrary"),
        name="gla",
    )(p, p, p, la, norm_g.reshape(1, GLA_WIDTH))


def _mixout_kernel(h_ref, u_ref, up_ref, un_ref, gla_ref, mod_ref, pw_ref, ps_ref, w_ref, g_ref, b_ref,
                   o_ref, ue_ref, *, n_batch, alpha):
    tb = SEQ_TOKENS
    i = pl.program_id(1)
    is_ctx = pl.program_id(0) == n_batch
    first = is_ctx | (i == 0)
    last = is_ctx | (i == pl.num_programs(1) - 1)
    u = u_ref[0].astype(F32)
    ue_ref[0:HALO, :] = jnp.where(first, 0.0, up_ref[0].astype(F32))
    ue_ref[HALO:HALO + tb, :] = u
    ue_ref[HALO + tb:, :] = jnp.where(last, 0.0, un_ref[0].astype(F32))

    pos = lax.broadcasted_iota(jnp.int32, (tb, 1), 0)
    room_lo = jnp.where(first, pos, tb)
    room_hi = jnp.where(last, tb - 1 - pos, tb)
    ys = []
    for g, w in enumerate(POOL_WINDOWS):
        cs = slice(g * POOL_GROUP, (g + 1) * POOL_GROUP)
        lo, hi = w // 2, w - 1 - w // 2
        s = ue_ref[HALO - lo:HALO - lo + tb, cs]
        for off in range(-lo + 1, hi + 1):
            s = s + ue_ref[HALO + off:HALO + off + tb, cs]
        cnt = (jnp.minimum(room_lo, lo) + jnp.minimum(room_hi, hi) + 1).astype(F32)
        pooled = s / cnt - u[:, cs]
        ys.append(_dot(pooled.astype(BF16), pw_ref[g]))
    pool_y = (jnp.concatenate(ys, axis=1) * ps_ref[...]).astype(BF16)
    y = _dot(pool_y, w_ref[:POOL_WIDTH, :]) + _dot(gla_ref[0], w_ref[POOL_WIDTH:, :])
    z = alpha * h_ref[0] + mod_ref[0, 5:6, :] * y
    o_ref[0] = _layernorm(z) * g_ref[...] + b_ref[...]


def _mixout(h, p, gla, mod, pool_w, pool_scale, w_out, ln_g, ln_b, *, n_batch, alpha, nseg):
    _, L, D = h.shape
    tb = SEQ_TOKENS
    nt = L // tb
    hb = tb // HALO
    return pl.pallas_call(
        functools.partial(_mixout_kernel, n_batch=n_batch, alpha=alpha),
        grid=(nseg, nt),
        in_specs=[
            pl.BlockSpec((1, tb, D), lambda b, i: (b, i, 0)),
            pl.BlockSpec((1, tb, POOL_WIDTH), lambda b, i: (b, i, 0)),
            pl.BlockSpec((1, HALO, POOL_WIDTH), lambda b, i: (b, jnp.maximum(i * hb - 1, 0), 0)),
            pl.BlockSpec((1, HALO, POOL_WIDTH), lambda b, i: (b, jnp.minimum((i + 1) * hb, nt * hb - 1), 0)),
            pl.BlockSpec((1, tb, GLA_WIDTH), lambda b, i: (b, i, 0)),
            pl.BlockSpec((1, N_MOD, D), lambda b, i: (b, 0, 0)),
            _resident(pool_w.shape),
            _resident((1, POOL_WIDTH)),
            _resident(w_out.shape),
            _resident((1, D)),
            _resident((1, D)),
        ],
        out_specs=pl.BlockSpec((1, tb, D), lambda b, i: (b, i, 0)),
        out_shape=jax.ShapeDtypeStruct((nseg, L, D), F32),
        scratch_shapes=[pltpu.VMEM((tb + 2 * HALO, POOL_WIDTH), F32)],
        compiler_params=_params("arbitrary", "arbitrary"),
        name="mixout",
    )(h, p, p, p, gla, mod, pool_w, pool_scale.reshape(1, POOL_WIDTH), w_out,
      ln_g.reshape(1, D), ln_b.reshape(1, D))


def _pos_embed_2d(L):
    rows = L // GRID_W
    r = jnp.repeat(jnp.arange(rows, dtype=F32), GRID_W)
    col = jnp.tile(jnp.arange(GRID_W, dtype=F32), rows)
    quarter = D_MODEL // 4
    omega = 1.0 / (10000.0 ** (jnp.arange(quarter, dtype=F32) / quarter))

    def enc(p):
        a = p[:, None] * omega
        return jnp.concatenate([jnp.sin(a), jnp.cos(a)], axis=-1)

    return jnp.concatenate([enc(r), enc(col)], axis=-1)


def kernel(x, c, ctx, c_ctx, w_ada, b_ada, ln_g, ln_b, ffa_w_in, ffa_w_out, mix_w_in, pool_w, pool_scale,
           gate_up_f, gate_bias_f, gate_up_b, gate_bias_b, gla_norm_g, mix_w_out, ffb_w_in, ffb_w_out):
    B, L, D = x.shape
    LC = ctx.shape[1]
    depth = w_ada.shape[0]
    assert D == D_MODEL and LC == SEQ_TOKENS and B * LC == L and L % FFN_TOKENS == 0
    alpha = (2.0 * depth) ** 0.25
    nseg = B + 1

    rows = -(-nseg // 8) * 8
    cond = jnp.concatenate([c, c_ctx[None, :], jnp.zeros((rows - nseg, D), F32)], axis=0)
    mod = _ada(cond, w_ada, b_ada)[:, :nseg].reshape(depth, nseg, N_MOD, D)

    h = _prenorm(x, _pos_embed_2d(L), ctx.reshape(1, L, D))

    zeros_gd = jnp.zeros((depth, D, GD_PAD - 2 * GATE_RANK), F32)
    w_mix_in = jnp.concatenate([mix_w_in, zeros_gd], axis=-1).astype(BF16)
    gup = jnp.zeros((depth, GD_PAD, 2 * GLA_KW), F32)
    gup = gup.at[:, :GATE_RANK, :GLA_KW].set(gate_up_f)
    gup = gup.at[:, GATE_RANK:2 * GATE_RANK, GLA_KW:].set(gate_up_b).astype(BF16)
    gbias = jnp.concatenate([gate_bias_f, gate_bias_b], axis=-1).reshape(depth, 1, 2 * GLA_KW)
    ffa_in, ffa_out = ffa_w_in.astype(BF16), ffa_w_out.astype(BF16)
    ffb_in, ffb_out = ffb_w_in.astype(BF16), ffb_w_out.astype(BF16)
    w_mix_out = mix_w_out.astype(BF16)
    pw = pool_w.astype(BF16)

    for l in range(depth):
        last = l == depth - 1
        h = _ffn(h, mod[l], ffa_in[l], ffa_out[l], ln_g[l, 0], ln_b[l, 0], k0=0, alpha=alpha, nseg=nseg)
        p, la = _inproj(h, mod[l], w_mix_in[l], gup[l], gbias[l])
        g = _gla(p, la, gla_norm_g[l], n_batch=B)
        nout = B if last else nseg
        h = _mixout(h, p, g, mod[l], pw[l], pool_scale[l], w_mix_out[l], ln_g[l, 1], ln_b[l, 1],
                    n_batch=B, alpha=alpha, nseg=nout)
        h = _ffn(h, mod[l], ffb_in[l], ffb_out[l], ln_g[l, 2], ln_b[l, 2], k0=6, alpha=alpha, nseg=nout)
    return h
```

```python
import functools

import jax
import jax.numpy as jnp
from jax import lax
from jax.experimental import pallas as pl
from jax.experimental.pallas import tpu as pltpu

F32 = jnp.float32
BF16 = jnp.bfloat16

D_MODEL = 1024
D_FF = 2816
N_MOD = 9
POOL_WINDOWS = (2, 4, 8, 16)
POOL_GROUP = 128
POOL_WIDTH = POOL_GROUP * len(POOL_WINDOWS)
GLA_HEADS = 4
GLA_DK = 64
GLA_DV = 128
GLA_KW = GLA_HEADS * GLA_DK
GLA_WIDTH = GLA_HEADS * GLA_DV
GATE_RANK = 16
GATE_TAU = 16.0
GLA_CHUNK = 64
GRID_W = 64
LN_EPS = 1e-6
MAIN_COLS = POOL_WIDTH + 2 * GLA_KW + 2 * GLA_WIDTH
P_COLS = POOL_WIDTH + 2 * GLA_WIDTH
QK_COLS = 3 * GLA_KW
GD_PAD = 256

VMEM_LIMIT_BYTES = 56 * 1024 * 1024
FFN_TOKENS = 512
FFN_CHUNK = 256
SEQ_TOKENS = 256
SUB_TOKENS = 128
CUM_TOKENS = 256
HALO = 16


def _dot(a, b):
    return jnp.dot(a, b, preferred_element_type=F32)


def _dot_nt(a, b):
    return lax.dot_general(a, b, (((1,), (1,)), ((), ())), preferred_element_type=F32)


def _dot_tn(a, b):
    return lax.dot_general(a, b, (((0,), (0,)), ((), ())), preferred_element_type=F32)


def _silu(x):
    return x / (1.0 + jnp.exp(-x))


def _layernorm(z):
    mu = jnp.mean(z, axis=-1, keepdims=True)
    zc = z - mu
    var = jnp.mean(zc * zc, axis=-1, keepdims=True)
    return zc * lax.rsqrt(var + LN_EPS)


def _params(*sem):
    return pltpu.CompilerParams(dimension_semantics=sem, vmem_limit_bytes=VMEM_LIMIT_BYTES)


def _resident(shape):
    nd = len(shape)
    return pl.BlockSpec(shape, lambda *_: (0,) * nd, pipeline_mode=pl.Buffered(1))


def _ada_kernel(c_ref, w_ref, b_ref, o_ref):
    s = _silu(c_ref[...])
    o_ref[0] = _dot(s.astype(BF16), w_ref[0].astype(BF16)) + b_ref[0]


def _ada(cond, w_ada, b_ada):
    depth, d, n = w_ada.shape
    rows = cond.shape[0]
    tn = 1024
    return pl.pallas_call(
        _ada_kernel,
        grid=(depth, n // tn),
        in_specs=[
            pl.BlockSpec((rows, d), lambda l, j: (0, 0)),
            pl.BlockSpec((1, d, tn), lambda l, j: (l, 0, j)),
            pl.BlockSpec((1, 1, tn), lambda l, j: (l, 0, j)),
        ],
        out_specs=pl.BlockSpec((1, rows, tn), lambda l, j: (l, 0, j)),
        out_shape=jax.ShapeDtypeStruct((depth, rows, n), F32),
        compiler_params=_params("arbitrary", "arbitrary"),
        name="ada",
    )(cond, w_ada, b_ada.reshape(depth, 1, n))


def _prenorm_kernel(x_ref, pos_ref, ctx_ref, o_ref, *, n_batch):
    is_ctx = pl.program_id(0) == n_batch
    z = jnp.where(is_ctx, ctx_ref[0], x_ref[0] + pos_ref[...])
    o_ref[0] = _layernorm(z)


def _prenorm(x, pos, ctx_flat):
    B, L, D = x.shape
    tm = FFN_TOKENS
    nt = L // tm
    return pl.pallas_call(
        functools.partial(_prenorm_kernel, n_batch=B),
        grid=(B + 1, nt),
        in_specs=[
            pl.BlockSpec((1, tm, D), lambda b, i: (jnp.minimum(b, B - 1), jnp.where(b == B, nt - 1, i), 0)),
            pl.BlockSpec((tm, D), lambda b, i: (jnp.where(b == B, nt - 1, i), 0)),
            pl.BlockSpec((1, tm, D), lambda b, i: (0, jnp.where(b == B, i, 0), 0)),
        ],
        out_specs=pl.BlockSpec((1, tm, D), lambda b, i: (b, i, 0)),
        out_shape=jax.ShapeDtypeStruct((B + 1, L, D), F32),
        compiler_params=_params("arbitrary", "arbitrary"),
        name="prenorm",
    )(x, pos, ctx_flat)


def _ffn_kernel(h_ref, mod_ref, w_in_ref, w_out_ref, g_ref, b_ref, o_ref, a_ref, *, k0, alpha):
    x = h_ref[0]
    shift = mod_ref[0, k0:k0 + 1, :]
    scale = mod_ref[0, k0 + 1:k0 + 2, :]
    gate = mod_ref[0, k0 + 2:k0 + 3, :]
    xm = (x * (1.0 + scale) + shift).astype(BF16)
    for j in range(D_FF // FFN_CHUNK):
        lo = j * FFN_CHUNK
        g = _dot(xm, w_in_ref[:, lo:lo + FFN_CHUNK])
        u = _dot(xm, w_in_ref[:, D_FF + lo:D_FF + lo + FFN_CHUNK])
        a_ref[:, lo:lo + FFN_CHUNK] = (_silu(g) * u).astype(BF16)
    y = _dot(a_ref[...], w_out_ref[...])
    z = alpha * x + (0.5 * gate) * y
    o_ref[0] = _layernorm(z) * g_ref[...] + b_ref[...]


def _ffn(h, mod, w_in, w_out, ln_g, ln_b, *, k0, alpha, nseg):
    _, L, D = h.shape
    tm = FFN_TOKENS
    return pl.pallas_call(
        functools.partial(_ffn_kernel, k0=k0, alpha=alpha),
        grid=(nseg, L // tm),
        in_specs=[
            pl.BlockSpec((1, tm, D), lambda b, i: (b, i, 0)),
            pl.BlockSpec((1, N_MOD, D), lambda b, i: (b, 0, 0)),
            _resident(w_in.shape),
            _resident(w_out.shape),
            _resident((1, D)),
            _resident((1, D)),
        ],
        out_specs=pl.BlockSpec((1, tm, D), lambda b, i: (b, i, 0)),
        out_shape=jax.ShapeDtypeStruct((nseg, L, D), F32),
        scratch_shapes=[pltpu.VMEM((tm, D_FF), BF16)],
        compiler_params=_params("arbitrary", "arbitrary"),
        name="ffn",
    )(h, mod, w_in, w_out, ln_g.reshape(1, D), ln_b.reshape(1, D))


def _inproj_kernel(h_ref, mod_ref, w_ref, gup_ref, gb_ref, trif_ref, trib_ref, p_ref, qk_ref, tot_ref):
    tm = h_ref.shape[1]
    x = h_ref[0]
    shift = mod_ref[0, 3:4, :]
    scale = mod_ref[0, 4:5, :]
    xm = (x * (1.0 + scale) + shift).astype(BF16)
    p_ref[0, :, 0:POOL_WIDTH] = _dot(xm, w_ref[:, 0:POOL_WIDTH]).astype(BF16)
    qk = _dot(xm, w_ref[:, POOL_WIDTH:POOL_WIDTH + 2 * GLA_KW])
    for j in (1, 2):
        p_ref[0, :, j * 512:(j + 1) * 512] = _dot(xm, w_ref[:, 512 + j * 512:1024 + j * 512]).astype(BF16)
    gd = _dot(xm, w_ref[:, MAIN_COLS:MAIN_COLS + GD_PAD])
    z = _dot(gd.astype(BF16), gup_ref[...]) + gb_ref[...]
    la = (jnp.minimum(z, 0.0) - jnp.log1p(jnp.exp(-jnp.abs(z)))) * (1.0 / GATE_TAU)
    la_hi = la.astype(BF16)
    la_lo = (la - la_hi.astype(F32)).astype(BF16)
    q = qk[:, :GLA_KW] * (GLA_DK ** -0.5)
    k = qk[:, GLA_KW:]
    for d, tri_ref in enumerate((trif_ref, trib_ref)):
        cs = slice(d * GLA_KW, (d + 1) * GLA_KW)
        parts = []
        for m in range(tm // CUM_TOKENS):
            rs = slice(m * CUM_TOKENS, (m + 1) * CUM_TOKENS)
            parts.append(_dot(tri_ref[...], la_hi[rs, cs]) + _dot(tri_ref[...], la_lo[rs, cs]))
        bcs = jnp.concatenate(parts, axis=0)
        tots = []
        for c in range(tm // GLA_CHUNK):
            e = c * GLA_CHUNK + (GLA_CHUNK - 1 if d == 0 else 0)
            t = bcs[e:e + 1, :]
            tot_ref[0, c:c + 1, cs] = t
            tots.append(jnp.broadcast_to(t, (GLA_CHUNK, GLA_KW)))
        tot = jnp.concatenate(tots, axis=0)
        base = d * QK_COLS
        qk_ref[0, :, base:base + GLA_KW] = (q * jnp.exp(bcs)).astype(BF16)
        qk_ref[0, :, base + GLA_KW:base + 2 * GLA_KW] = (k * jnp.exp(-bcs)).astype(BF16)
        qk_ref[0, :, base + 2 * GLA_KW:base + 3 * GLA_KW] = (k * jnp.exp(tot - bcs)).astype(BF16)


def _inproj(h, mod, w_all, gup, gbias, tri_f, tri_b):
    nseg, L, D = h.shape
    tm = FFN_TOKENS
    return pl.pallas_call(
        _inproj_kernel,
        grid=(nseg, L // tm),
        in_specs=[
            pl.BlockSpec((1, tm, D), lambda b, i: (b, i, 0)),
            pl.BlockSpec((1, N_MOD, D), lambda b, i: (b, 0, 0)),
            _resident(w_all.shape),
            _resident(gup.shape),
            _resident(gbias.shape),
            _resident(tri_f.shape),
            _resident(tri_b.shape),
        ],
        out_specs=[
            pl.BlockSpec((1, tm, P_COLS), lambda b, i: (b, i, 0)),
            pl.BlockSpec((1, tm, 2 * QK_COLS), lambda b, i: (b, i, 0)),
            pl.BlockSpec((1, tm // GLA_CHUNK, 2 * GLA_KW), lambda b, i: (b, i, 0)),
        ],
        out_shape=[
            jax.ShapeDtypeStruct((nseg, L, P_COLS), BF16),
            jax.ShapeDtypeStruct((nseg, L, 2 * QK_COLS), BF16),
            jax.ShapeDtypeStruct((nseg, L // GLA_CHUNK, 2 * GLA_KW), F32),
        ],
        compiler_params=_params("arbitrary", "arbitrary"),
        name="inproj",
    )(h, mod, w_all, gup, gbias, tri_f, tri_b)


def _gla_block(qk_ref, v_ref, tot_ref, st_ref, *, fwd, odd_block):
    tb = SEQ_TOKENS
    pair = 2 * GLA_DK
    d = 0 if fwd else 1
    qk = qk_ref[0]
    v = v_ref[0]
    lane = lax.broadcasted_iota(jnp.int32, (pair, pair), 1)
    first_head = lane < GLA_DK
    head_mask = [jnp.where(first_head, 1.0, 0.0).astype(BF16), jnp.where(first_head, 0.0, 1.0).astype(BF16)]
    srow = lax.broadcasted_iota(jnp.int32, (2 * SUB_TOKENS, SUB_TOKENS), 0) % SUB_TOKENS
    scol = lax.broadcasted_iota(jnp.int32, (2 * SUB_TOKENS, SUB_TOKENS), 1)
    keep = ((srow // GLA_CHUNK) == (scol // GLA_CHUNK)) & ((scol <= srow) if fwd else (scol >= srow))

    nchunk = tb // GLA_CHUNK
    st = [st_ref[p * pair:(p + 1) * pair, :] for p in range(GLA_HEADS // 2)]
    st_start = {}
    for c in (range(nchunk) if fwd else range(nchunk - 1, -1, -1)):
        rs = slice(c * GLA_CHUNK, (c + 1) * GLA_CHUNK)
        for p in range(GLA_HEADS // 2):
            st_start[c, p] = st[p].astype(BF16)
            k_end = qk[rs, 2 * GLA_KW + p * pair:2 * GLA_KW + (p + 1) * pair]
            upd = [_dot_tn(v[rs, (2 * p + hh) * GLA_DV:(2 * p + hh + 1) * GLA_DV], k_end) for hh in range(2)]
            ls = slice(d * GLA_KW + p * pair, d * GLA_KW + (p + 1) * pair)
            decay = jnp.exp(jnp.where(odd_block, tot_ref[0, nchunk + c:nchunk + c + 1, ls],
                                      tot_ref[0, c:c + 1, ls]))
            st[p] = st[p] * decay + jnp.where(first_head, upd[0], upd[1])
    for p in range(GLA_HEADS // 2):
        st_ref[p * pair:(p + 1) * pair, :] = st[p]

    zeros_half = jnp.zeros((GLA_CHUNK, pair), BF16)
    outs = []
    for sb in range(tb // SUB_TOKENS):
        rs = slice(sb * SUB_TOKENS, (sb + 1) * SUB_TOKENS)
        o_heads = []
        for p in range(GLA_HEADS // 2):
            qp = qk[rs, p * pair:(p + 1) * pair]
            q_h = [qp * head_mask[0], qp * head_mask[1]]
            k_in = qk[rs, GLA_KW + p * pair:GLA_KW + (p + 1) * pair]
            sc = _dot_nt(jnp.concatenate(q_h, axis=0), k_in)
            sc = jnp.where(keep, sc, 0.0).astype(BF16)
            c0 = sb * (SUB_TOKENS // GLA_CHUNK)
            st_cat = jnp.concatenate([st_start[c0, p], st_start[c0 + 1, p]], axis=1)
            for hh in range(2):
                h = 2 * p + hh
                qh = q_h[hh]
                q_inter = jnp.concatenate(
                    [jnp.concatenate([qh[:GLA_CHUNK], zeros_half], axis=0),
                     jnp.concatenate([zeros_half, qh[GLA_CHUNK:]], axis=0)], axis=1)
                o_heads.append(_dot(sc[hh * SUB_TOKENS:(hh + 1) * SUB_TOKENS], v[rs, h * GLA_DV:(h + 1) * GLA_DV])
                               + _dot_nt(q_inter, st_cat))
        outs.append((sb, jnp.concatenate(o_heads, axis=1)))
    return outs


def _gla_kernel(qkf_ref, qkb_ref, v_ref, r_ref, tot_ref, ng_ref, o_ref, of_ref, st_ref, *, nb):
    b = pl.program_id(0)
    ph = pl.program_id(1)
    j = pl.program_id(2)
    jb = jnp.where(ph == 0, j, jnp.where(j == 0, 0, nb + 1 - j))
    t = jnp.where(jb == 0, b, jb - 1)
    odd_block = (t % 2) == 1
    base = pl.multiple_of(jb * SEQ_TOKENS, SEQ_TOKENS)

    @pl.when(j == 0)
    def _():
        st_ref[...] = jnp.zeros_like(st_ref)

    @pl.when(ph == 0)
    def _():
        for sb, o in _gla_block(qkf_ref, v_ref, tot_ref, st_ref, fwd=True, odd_block=odd_block):
            of_ref[pl.ds(base + sb * SUB_TOKENS, SUB_TOKENS), :] = o

    @pl.when(ph == 1)
    def _():
        for sb, o in _gla_block(qkb_ref, v_ref, tot_ref, st_ref, fwd=False, odd_block=odd_block):
            rs = slice(sb * SUB_TOKENS, (sb + 1) * SUB_TOKENS)
            o = o + of_ref[pl.ds(base + sb * SUB_TOKENS, SUB_TOKENS), :]
            normed = []
            for h in range(GLA_HEADS):
                oh = o[:, h * GLA_DV:(h + 1) * GLA_DV]
                normed.append(oh * lax.rsqrt(jnp.mean(oh * oh, axis=-1, keepdims=True) + LN_EPS))
            on = jnp.concatenate(normed, axis=1) * ng_ref[...]
            o_ref[0, rs, :] = (on * _silu(r_ref[0, rs, :].astype(F32))).astype(BF16)


def _gla(p, qk, tot, norm_g, *, n_batch):
    nseg, L, _ = p.shape
    tb = SEQ_TOKENS
    nb = L // tb

    def blk(b, ph, j):
        jb = jnp.where(ph == 0, j, jnp.where(j == 0, 0, nb + 1 - j))
        return jnp.where(jb == 0, n_batch, b), jnp.where(jb == 0, b, jb - 1)

    def col(cidx):
        def index_map(b, ph, j):
            seg, t = blk(b, ph, j)
            return seg, t, cidx
        return index_map

    def one_pass_only(phase, cidx):
        def index_map(b, ph, j):
            seg, t = blk(b, ph, j)
            return jnp.where(ph == phase, seg, n_batch), jnp.where(ph == phase, t, b), cidx
        return index_map

    def tot_map(b, ph, j):
        seg, t = blk(b, ph, j)
        return seg, t // 2, 0

    return pl.pallas_call(
        functools.partial(_gla_kernel, nb=nb),
        grid=(n_batch, 2, nb + 1),
        in_specs=[
            pl.BlockSpec((1, tb, QK_COLS), one_pass_only(0, 0)),
            pl.BlockSpec((1, tb, QK_COLS), one_pass_only(1, 1)),
            pl.BlockSpec((1, tb, GLA_WIDTH), col(1)),
            pl.BlockSpec((1, tb, GLA_WIDTH), one_pass_only(1, 2)),
            pl.BlockSpec((1, 2 * tb // GLA_CHUNK, 2 * GLA_KW), tot_map),
            _resident((1, GLA_WIDTH)),
        ],
        out_specs=pl.BlockSpec((1, tb, GLA_WIDTH), one_pass_only(1, 0)),
        out_shape=jax.ShapeDtypeStruct((nseg, L, GLA_WIDTH), BF16),
        scratch_shapes=[
            pltpu.VMEM(((nb + 1) * tb, GLA_WIDTH), F32),
            pltpu.VMEM(((GLA_HEADS // 2) * GLA_DV, 2 * GLA_DK), F32),
        ],
        compiler_params=_params("arbitrary", "arbitrary", "arbitrary"),
        name="gla",
    )(qk, qk, p, p, tot, norm_g.reshape(1, GLA_WIDTH))


def _mixout_kernel(h_ref, u_ref, up_ref, un_ref, gla_ref, mod_ref, pw_ref, ps_ref, w_ref, g_ref, b_ref,
                   o_ref, ue_ref, *, n_batch, alpha):
    tb = SEQ_TOKENS
    i = pl.program_id(1)
    is_ctx = pl.program_id(0) == n_batch
    first = is_ctx | (i == 0)
    last = is_ctx | (i == pl.num_programs(1) - 1)
    u = u_ref[0].astype(F32)
    ue_ref[0:HALO, :] = jnp.where(first, 0.0, up_ref[0].astype(F32))
    ue_ref[HALO:HALO + tb, :] = u
    ue_ref[HALO + tb:, :] = jnp.where(last, 0.0, un_ref[0].astype(F32))

    pos = lax.broadcasted_iota(jnp.int32, (tb, 1), 0)
    room_lo = jnp.where(first, pos, tb)
    room_hi = jnp.where(last, tb - 1 - pos, tb)
    ys = []
    for g, w in enumerate(POOL_WINDOWS):
        cs = slice(g * POOL_GROUP, (g + 1) * POOL_GROUP)
        lo, hi = w // 2, w - 1 - w // 2
        s = ue_ref[HALO - lo:HALO - lo + tb, cs]
        for off in range(-lo + 1, hi + 1):
            s = s + ue_ref[HALO + off:HALO + off + tb, cs]
        cnt = (jnp.minimum(room_lo, lo) + jnp.minimum(room_hi, hi) + 1).astype(F32)
        pooled = s / cnt - u[:, cs]
        ys.append(_dot(pooled.astype(BF16), pw_ref[g]))
    pool_y = (jnp.concatenate(ys, axis=1) * ps_ref[...]).astype(BF16)
    y = _dot(pool_y, w_ref[:POOL_WIDTH, :]) + _dot(gla_ref[0], w_ref[POOL_WIDTH:, :])
    z = alpha * h_ref[0] + mod_ref[0, 5:6, :] * y
    o_ref[0] = _layernorm(z) * g_ref[...] + b_ref[...]


def _mixout(h, p, gla, mod, pool_w, pool_scale, w_out, ln_g, ln_b, *, n_batch, alpha, nseg):
    _, L, D = h.shape
    tb = SEQ_TOKENS
    nt = L // tb
    hb = tb // HALO
    return pl.pallas_call(
        functools.partial(_mixout_kernel, n_batch=n_batch, alpha=alpha),
        grid=(nseg, nt),
        in_specs=[
            pl.BlockSpec((1, tb, D), lambda b, i: (b, i, 0)),
            pl.BlockSpec((1, tb, POOL_WIDTH), lambda b, i: (b, i, 0)),
            pl.BlockSpec((1, HALO, POOL_WIDTH), lambda b, i: (b, jnp.maximum(i * hb - 1, 0), 0)),
            pl.BlockSpec((1, HALO, POOL_WIDTH), lambda b, i: (b, jnp.minimum((i + 1) * hb, nt * hb - 1), 0)),
            pl.BlockSpec((1, tb, GLA_WIDTH), lambda b, i: (b, i, 0)),
            pl.BlockSpec((1, N_MOD, D), lambda b, i: (b, 0, 0)),
            _resident(pool_w.shape),
            _resident((1, POOL_WIDTH)),
            _resident(w_out.shape),
            _resident((1, D)),
            _resident((1, D)),
        ],
        out_specs=pl.BlockSpec((1, tb, D), lambda b, i: (b, i, 0)),
        out_shape=jax.ShapeDtypeStruct((nseg, L, D), F32),
        scratch_shapes=[pltpu.VMEM((tb + 2 * HALO, POOL_WIDTH), F32)],
        compiler_params=_params("arbitrary", "arbitrary"),
        name="mixout",
    )(h, p, p, p, gla, mod, pool_w, pool_scale.reshape(1, POOL_WIDTH), w_out,
      ln_g.reshape(1, D), ln_b.reshape(1, D))


def _pos_embed_2d(L):
    rows = L // GRID_W
    r = jnp.repeat(jnp.arange(rows, dtype=F32), GRID_W)
    col = jnp.tile(jnp.arange(GRID_W, dtype=F32), rows)
    quarter = D_MODEL // 4
    omega = 1.0 / (10000.0 ** (jnp.arange(quarter, dtype=F32) / quarter))

    def enc(p):
        a = p[:, None] * omega
        return jnp.concatenate([jnp.sin(a), jnp.cos(a)], axis=-1)

    return jnp.concatenate([enc(r), enc(col)], axis=-1)


def kernel(x, c, ctx, c_ctx, w_ada, b_ada, ln_g, ln_b, ffa_w_in, ffa_w_out, mix_w_in, pool_w, pool_scale,
           gate_up_f, gate_bias_f, gate_up_b, gate_bias_b, gla_norm_g, mix_w_out, ffb_w_in, ffb_w_out):
    B, L, D = x.shape
    LC = ctx.shape[1]
    depth = w_ada.shape[0]
    assert D == D_MODEL and LC == SEQ_TOKENS and B * LC == L and L % FFN_TOKENS == 0
    alpha = (2.0 * depth) ** 0.25
    nseg = B + 1

    rows = -(-nseg // 8) * 8
    cond = jnp.concatenate([c, c_ctx[None, :], jnp.zeros((rows - nseg, D), F32)], axis=0)
    mod = _ada(cond, w_ada, b_ada)[:, :nseg].reshape(depth, nseg, N_MOD, D)

    h = _prenorm(x, _pos_embed_2d(L), ctx.reshape(1, L, D))

    zeros_gd = jnp.zeros((depth, D, GD_PAD - 2 * GATE_RANK), F32)
    w_mix_in = jnp.concatenate([mix_w_in, zeros_gd], axis=-1).astype(BF16)
    gup = jnp.zeros((depth, GD_PAD, 2 * GLA_KW), F32)
    gup = gup.at[:, :GATE_RANK, :GLA_KW].set(gate_up_f)
    gup = gup.at[:, GATE_RANK:2 * GATE_RANK, GLA_KW:].set(gate_up_b).astype(BF16)
    gbias = jnp.concatenate([gate_bias_f, gate_bias_b], axis=-1).reshape(depth, 1, 2 * GLA_KW)
    ffa_in, ffa_out = ffa_w_in.astype(BF16), ffa_w_out.astype(BF16)
    ffb_in, ffb_out = ffb_w_in.astype(BF16), ffb_w_out.astype(BF16)
    w_mix_out = mix_w_out.astype(BF16)
    pw = pool_w.astype(BF16)
    ti = jnp.arange(CUM_TOKENS)
    same_chunk = (ti[:, None] // GLA_CHUNK) == (ti[None, :] // GLA_CHUNK)
    tri_f = (same_chunk & (ti[None, :] <= ti[:, None])).astype(BF16)
    tri_b = (same_chunk & (ti[None, :] >= ti[:, None])).astype(BF16)

    for l in range(depth):
        last = l == depth - 1
        h = _ffn(h, mod[l], ffa_in[l], ffa_out[l], ln_g[l, 0], ln_b[l, 0], k0=0, alpha=alpha, nseg=nseg)
        p, qk, tot = _inproj(h, mod[l], w_mix_in[l], gup[l], gbias[l], tri_f, tri_b)
        g = _gla(p, qk, tot, gla_norm_g[l], n_batch=B)
        nout = B if last else nseg
        h = _mixout(h, p, g, mod[l], pw[l], pool_scale[l], w_mix_out[l], ln_g[l, 1], ln_b[l, 1],
                    n_batch=B, alpha=alpha, nseg=nout)
        h = _ffn(h, mod[l], ffb_in[l], ffb_out[l], ln_g[l, 2], ln_b[l, 2], k0=6, alpha=alpha, nseg=nout)
    return h
```

```python
import functools

import jax
import jax.numpy as jnp
from jax import lax
from jax.experimental import pallas as pl
from jax.experimental.pallas import tpu as pltpu

F32 = jnp.float32
BF16 = jnp.bfloat16

D_MODEL = 1024
D_FF = 2816
N_MOD = 9
POOL_WINDOWS = (2, 4, 8, 16)
POOL_GROUP = 128
POOL_WIDTH = POOL_GROUP * len(POOL_WINDOWS)
GLA_HEADS = 4
GLA_DK = 64
GLA_DV = 128
GLA_KW = GLA_HEADS * GLA_DK
GLA_WIDTH = GLA_HEADS * GLA_DV
GATE_RANK = 16
GATE_TAU = 16.0
GLA_CHUNK = 64
GRID_W = 64
LN_EPS = 1e-6
MAIN_COLS = POOL_WIDTH + 2 * GLA_KW + 2 * GLA_WIDTH
P_COLS = POOL_WIDTH + 2 * GLA_WIDTH
QK_COLS = 3 * GLA_KW
GD_PAD = 256

VMEM_LIMIT_BYTES = 56 * 1024 * 1024
FFN_TOKENS = 512
FFN_CHUNK = 256
FFN_ROWS = 256
SEQ_TOKENS = 256
GLA_TOKENS = 1024
SUB_TOKENS = 128
CUM_TOKENS = 256
POOL_TOKENS = 256
HALO = 16


def _dot(a, b):
    return jnp.dot(a, b, preferred_element_type=F32)


def _dot_nt(a, b):
    return lax.dot_general(a, b, (((1,), (1,)), ((), ())), preferred_element_type=F32)


def _dot_tn(a, b):
    return lax.dot_general(a, b, (((0,), (0,)), ((), ())), preferred_element_type=F32)


def _silu(x):
    return x / (1.0 + jnp.exp(-x))


def _layernorm(z):
    mu = jnp.mean(z, axis=-1, keepdims=True)
    zc = z - mu
    var = jnp.mean(zc * zc, axis=-1, keepdims=True)
    return zc * lax.rsqrt(var + LN_EPS)


def _params(*sem):
    return pltpu.CompilerParams(dimension_semantics=sem, vmem_limit_bytes=VMEM_LIMIT_BYTES)


def _resident(shape):
    nd = len(shape)
    return pl.BlockSpec(shape, lambda *_: (0,) * nd, pipeline_mode=pl.Buffered(1))


def _ada_kernel(c_ref, w_ref, b_ref, o_ref):
    s = _silu(c_ref[...])
    o_ref[0] = _dot(s.astype(BF16), w_ref[0].astype(BF16)) + b_ref[0]


def _ada(cond, w_ada, b_ada):
    depth, d, n = w_ada.shape
    rows = cond.shape[0]
    tn = 1024
    return pl.pallas_call(
        _ada_kernel,
        grid=(depth, n // tn),
        in_specs=[
            pl.BlockSpec((rows, d), lambda l, j: (0, 0)),
            pl.BlockSpec((1, d, tn), lambda l, j: (l, 0, j)),
            pl.BlockSpec((1, 1, tn), lambda l, j: (l, 0, j)),
        ],
        out_specs=pl.BlockSpec((1, rows, tn), lambda l, j: (l, 0, j)),
        out_shape=jax.ShapeDtypeStruct((depth, rows, n), F32),
        compiler_params=_params("arbitrary", "arbitrary"),
        name="ada",
    )(cond, w_ada, b_ada.reshape(depth, 1, n))


def _prenorm_kernel(x_ref, pos_ref, ctx_ref, o_ref, *, n_batch):
    is_ctx = pl.program_id(0) == n_batch
    z = jnp.where(is_ctx, ctx_ref[0], x_ref[0] + pos_ref[...])
    o_ref[0] = _layernorm(z)


def _prenorm(x, pos, ctx_flat):
    B, L, D = x.shape
    tm = FFN_TOKENS
    nt = L // tm
    return pl.pallas_call(
        functools.partial(_prenorm_kernel, n_batch=B),
        grid=(B + 1, nt),
        in_specs=[
            pl.BlockSpec((1, tm, D), lambda b, i: (jnp.minimum(b, B - 1), jnp.where(b == B, nt - 1, i), 0)),
            pl.BlockSpec((tm, D), lambda b, i: (jnp.where(b == B, nt - 1, i), 0)),
            pl.BlockSpec((1, tm, D), lambda b, i: (0, jnp.where(b == B, i, 0), 0)),
        ],
        out_specs=pl.BlockSpec((1, tm, D), lambda b, i: (b, i, 0)),
        out_shape=jax.ShapeDtypeStruct((B + 1, L, D), F32),
        compiler_params=_params("arbitrary", "arbitrary"),
        name="prenorm",
    )(x, pos, ctx_flat)


def _ffn_kernel(h_ref, mod_ref, w_in_ref, w_out_ref, g_ref, b_ref, o_ref, a_ref, *, k0, alpha):
    x = h_ref[0]
    shift = mod_ref[0, k0:k0 + 1, :]
    scale = mod_ref[0, k0 + 1:k0 + 2, :]
    gate = mod_ref[0, k0 + 2:k0 + 3, :]
    for r in range(x.shape[0] // FFN_ROWS):
        rs = slice(r * FFN_ROWS, (r + 1) * FFN_ROWS)
        xr = x[rs]
        xm = (xr * (1.0 + scale) + shift).astype(BF16)
        for j in range(D_FF // FFN_CHUNK):
            lo = j * FFN_CHUNK
            g = _dot(xm, w_in_ref[:, lo:lo + FFN_CHUNK])
            u = _dot(xm, w_in_ref[:, D_FF + lo:D_FF + lo + FFN_CHUNK])
            a_ref[rs, lo:lo + FFN_CHUNK] = (_silu(g) * u).astype(BF16)
        y = _dot(a_ref[rs, :], w_out_ref[...])
        z = alpha * xr + (0.5 * gate) * y
        o_ref[0, rs, :] = _layernorm(z) * g_ref[...] + b_ref[...]


def _ffn(h, mod, w_in, w_out, ln_g, ln_b, *, k0, alpha, nseg):
    _, L, D = h.shape
    tm = FFN_TOKENS
    return pl.pallas_call(
        functools.partial(_ffn_kernel, k0=k0, alpha=alpha),
        grid=(nseg, L // tm),
        in_specs=[
            pl.BlockSpec((1, tm, D), lambda b, i: (b, i, 0)),
            pl.BlockSpec((1, N_MOD, D), lambda b, i: (b, 0, 0)),
            _resident(w_in.shape),
            _resident(w_out.shape),
            _resident((1, D)),
            _resident((1, D)),
        ],
        out_specs=pl.BlockSpec((1, tm, D), lambda b, i: (b, i, 0)),
        out_shape=jax.ShapeDtypeStruct((nseg, L, D), F32),
        scratch_shapes=[pltpu.VMEM((tm, D_FF), BF16)],
        compiler_params=_params("arbitrary", "arbitrary"),
        name="ffn",
    )(h, mod, w_in, w_out, ln_g.reshape(1, D), ln_b.reshape(1, D))


def _inproj_kernel(h_ref, mod_ref, w_ref, gup_ref, gb_ref, trif_ref, trib_ref, p_ref, qk_ref, tot_ref):
    tm = h_ref.shape[1]
    x = h_ref[0]
    shift = mod_ref[0, 3:4, :]
    scale = mod_ref[0, 4:5, :]
    xm = (x * (1.0 + scale) + shift).astype(BF16)
    p_ref[0, :, 0:POOL_WIDTH] = _dot(xm, w_ref[:, 0:POOL_WIDTH]).astype(BF16)
    qk = _dot(xm, w_ref[:, POOL_WIDTH:POOL_WIDTH + 2 * GLA_KW])
    for j in (1, 2):
        p_ref[0, :, j * 512:(j + 1) * 512] = _dot(xm, w_ref[:, 512 + j * 512:1024 + j * 512]).astype(BF16)
    gd = _dot(xm, w_ref[:, MAIN_COLS:MAIN_COLS + GD_PAD])
    z = _dot(gd.astype(BF16), gup_ref[...]) + gb_ref[...]
    la = (jnp.minimum(z, 0.0) - jnp.log1p(jnp.exp(-jnp.abs(z)))) * (1.0 / GATE_TAU)
    la_hi = la.astype(BF16)
    la_lo = (la - la_hi.astype(F32)).astype(BF16)
    q = qk[:, :GLA_KW] * (GLA_DK ** -0.5)
    k = qk[:, GLA_KW:]
    for d, tri_ref in enumerate((trif_ref, trib_ref)):
        cs = slice(d * GLA_KW, (d + 1) * GLA_KW)
        parts = []
        for m in range(tm // CUM_TOKENS):
            rs = slice(m * CUM_TOKENS, (m + 1) * CUM_TOKENS)
            parts.append(_dot(tri_ref[...], la_hi[rs, cs]) + _dot(tri_ref[...], la_lo[rs, cs]))
        bcs = jnp.concatenate(parts, axis=0)
        tots = []
        for c in range(tm // GLA_CHUNK):
            e = c * GLA_CHUNK + (GLA_CHUNK - 1 if d == 0 else 0)
            t = bcs[e:e + 1, :]
            tot_ref[0, c:c + 1, cs] = t
            tots.append(jnp.broadcast_to(t, (GLA_CHUNK, GLA_KW)))
        tot = jnp.concatenate(tots, axis=0)
        base = d * QK_COLS
        qk_ref[0, :, base:base + GLA_KW] = (q * jnp.exp(bcs)).astype(BF16)
        qk_ref[0, :, base + GLA_KW:base + 2 * GLA_KW] = (k * jnp.exp(-bcs)).astype(BF16)
        qk_ref[0, :, base + 2 * GLA_KW:base + 3 * GLA_KW] = (k * jnp.exp(tot - bcs)).astype(BF16)


def _inproj(h, mod, w_all, gup, gbias, tri_f, tri_b):
    nseg, L, D = h.shape
    tm = FFN_TOKENS
    return pl.pallas_call(
        _inproj_kernel,
        grid=(nseg, L // tm),
        in_specs=[
            pl.BlockSpec((1, tm, D), lambda b, i: (b, i, 0)),
            pl.BlockSpec((1, N_MOD, D), lambda b, i: (b, 0, 0)),
            _resident(w_all.shape),
            _resident(gup.shape),
            _resident(gbias.shape),
            _resident(tri_f.shape),
            _resident(tri_b.shape),
        ],
        out_specs=[
            pl.BlockSpec((1, tm, P_COLS), lambda b, i: (b, i, 0)),
            pl.BlockSpec((1, tm, 2 * QK_COLS), lambda b, i: (b, i, 0)),
            pl.BlockSpec((1, tm // GLA_CHUNK, 2 * GLA_KW), lambda b, i: (b, i, 0)),
        ],
        out_shape=[
            jax.ShapeDtypeStruct((nseg, L, P_COLS), BF16),
            jax.ShapeDtypeStruct((nseg, L, 2 * QK_COLS), BF16),
            jax.ShapeDtypeStruct((nseg, L // GLA_CHUNK, 2 * GLA_KW), F32),
        ],
        compiler_params=_params("arbitrary", "arbitrary"),
        name="inproj",
    )(h, mod, w_all, gup, gbias, tri_f, tri_b)


def _gla_block(qk_ref, v_ref, tot_row, st_ref, *, fwd, tokens):
    pair = 2 * GLA_DK
    d = 0 if fwd else 1
    lane = lax.broadcasted_iota(jnp.int32, (pair, pair), 1)
    first_head = lane < GLA_DK
    head_mask = [jnp.where(first_head, 1.0, 0.0).astype(BF16), jnp.where(first_head, 0.0, 1.0).astype(BF16)]
    srow = lax.broadcasted_iota(jnp.int32, (2 * SUB_TOKENS, SUB_TOKENS), 0) % SUB_TOKENS
    scol = lax.broadcasted_iota(jnp.int32, (2 * SUB_TOKENS, SUB_TOKENS), 1)
    keep = ((srow // GLA_CHUNK) == (scol // GLA_CHUNK)) & ((scol <= srow) if fwd else (scol >= srow))

    nchunk = tokens // GLA_CHUNK
    st = [st_ref[p * pair:(p + 1) * pair, :] for p in range(GLA_HEADS // 2)]
    st_start = {}
    for c in (range(nchunk) if fwd else range(nchunk - 1, -1, -1)):
        rs = slice(c * GLA_CHUNK, (c + 1) * GLA_CHUNK)
        for p in range(GLA_HEADS // 2):
            st_start[c, p] = st[p].astype(BF16)
            k_end = qk_ref[0, rs, 2 * GLA_KW + p * pair:2 * GLA_KW + (p + 1) * pair]
            upd = _dot_tn(v_ref[0, rs, 2 * p * GLA_DV:(2 * p + 2) * GLA_DV], k_end)
            decay = jnp.exp(tot_row(c, slice(d * GLA_KW + p * pair, d * GLA_KW + (p + 1) * pair)))
            st[p] = st[p] * decay + jnp.where(first_head, upd[:GLA_DV], upd[GLA_DV:])
    for p in range(GLA_HEADS // 2):
        st_ref[p * pair:(p + 1) * pair, :] = st[p]

    zeros_half = jnp.zeros((GLA_CHUNK, pair), BF16)
    for sb in range(tokens // SUB_TOKENS):
        rs = slice(sb * SUB_TOKENS, (sb + 1) * SUB_TOKENS)
        o_heads = []
        for p in range(GLA_HEADS // 2):
            qp = qk_ref[0, rs, p * pair:(p + 1) * pair]
            q_h = [qp * head_mask[0], qp * head_mask[1]]
            k_in = qk_ref[0, rs, GLA_KW + p * pair:GLA_KW + (p + 1) * pair]
            sc = _dot_nt(jnp.concatenate(q_h, axis=0), k_in)
            sc = jnp.where(keep, sc, 0.0).astype(BF16)
            c0 = sb * (SUB_TOKENS // GLA_CHUNK)
            st_cat = jnp.concatenate([st_start[c0, p], st_start[c0 + 1, p]], axis=1)
            q_inter = jnp.concatenate(
                [jnp.concatenate([q_h[0][:GLA_CHUNK], zeros_half, q_h[1][:GLA_CHUNK], zeros_half], axis=0),
                 jnp.concatenate([zeros_half, q_h[0][GLA_CHUNK:], zeros_half, q_h[1][GLA_CHUNK:]], axis=0)], axis=1)
            inter = _dot_nt(q_inter, st_cat)
            for hh in range(2):
                h = 2 * p + hh
                hs = slice(hh * SUB_TOKENS, (hh + 1) * SUB_TOKENS)
                o_heads.append(_dot(sc[hs], v_ref[0, rs, h * GLA_DV:(h + 1) * GLA_DV]) + inter[hs])
        yield sb, jnp.concatenate(o_heads, axis=1)


def _gla_kernel(*refs, tokens, nblk, is_ctx):
    if is_ctx:
        qkf_ref, qkb_ref, v_ref, r_ref, tot_ref, ng_ref, o_ref, sfin_ref, of_ref, st_ref = refs
    else:
        qkf_ref, qkb_ref, v_ref, r_ref, tot_ref, s0_ref, ng_ref, o_ref, of_ref, st_ref = refs
    ph = pl.program_id(1)
    j = pl.program_id(2)
    jb = jnp.where(ph == 0, j, nblk - 1 - j)
    base = pl.multiple_of(jb * tokens, tokens)
    nchunk = tokens // GLA_CHUNK

    @pl.when(j == 0)
    def _():
        st_ref[...] = jnp.zeros_like(st_ref) if is_ctx else s0_ref[0, 0]

    if is_ctx:
        odd = (pl.program_id(0) % 2) == 1

        def tot_row(c, ls):
            return jnp.where(odd, tot_ref[0, nchunk + c:nchunk + c + 1, ls], tot_ref[0, c:c + 1, ls])
    else:
        def tot_row(c, ls):
            return tot_ref[0, c:c + 1, ls]

    @pl.when(ph == 0)
    def _():
        for sb, o in _gla_block(qkf_ref, v_ref, tot_row, st_ref, fwd=True, tokens=tokens):
            of_ref[pl.ds(base + sb * SUB_TOKENS, SUB_TOKENS), :] = o

    @pl.when(ph == 1)
    def _():
        for sb, o in _gla_block(qkb_ref, v_ref, tot_row, st_ref, fwd=False, tokens=tokens):
            rs = slice(sb * SUB_TOKENS, (sb + 1) * SUB_TOKENS)
            o = o + of_ref[pl.ds(base + sb * SUB_TOKENS, SUB_TOKENS), :]
            normed = []
            for h in range(GLA_HEADS):
                oh = o[:, h * GLA_DV:(h + 1) * GLA_DV]
                normed.append(oh * lax.rsqrt(jnp.mean(oh * oh, axis=-1, keepdims=True) + LN_EPS))
            on = jnp.concatenate(normed, axis=1) * ng_ref[...]
            o_ref[0, rs, :] = (on * _silu(r_ref[0, rs, :].astype(F32))).astype(BF16)

    if is_ctx:
        @pl.when(j == nblk - 1)
        def _():
            sfin_ref[0, 0] = st_ref[...]


def _gla_ctx(p, qk, tot, norm_g, *, n_batch):
    _, L, _ = p.shape
    tb = SEQ_TOKENS
    st_shape = ((GLA_HEADS // 2) * GLA_DV, 2 * GLA_DK)
    return pl.pallas_call(
        functools.partial(_gla_kernel, tokens=tb, nblk=1, is_ctx=True),
        grid=(n_batch, 2, 1),
        in_specs=[
            pl.BlockSpec((1, tb, QK_COLS), lambda b, ph, j: (n_batch, b, 0)),
            pl.BlockSpec((1, tb, QK_COLS), lambda b, ph, j: (n_batch, b, 1)),
            pl.BlockSpec((1, tb, GLA_WIDTH), lambda b, ph, j: (n_batch, b, 1)),
            pl.BlockSpec((1, tb, GLA_WIDTH), lambda b, ph, j: (n_batch, b, 2)),
            pl.BlockSpec((1, 2 * tb // GLA_CHUNK, 2 * GLA_KW), lambda b, ph, j: (n_batch, b // 2, 0)),
            _resident((1, GLA_WIDTH)),
        ],
        out_specs=[
            pl.BlockSpec((1, tb, GLA_WIDTH), lambda b, ph, j: (0, b, 0)),
            pl.BlockSpec((1, 1) + st_shape, lambda b, ph, j: (b, ph, 0, 0)),
        ],
        out_shape=[
            jax.ShapeDtypeStruct((1, L, GLA_WIDTH), BF16),
            jax.ShapeDtypeStruct((n_batch, 2) + st_shape, F32),
        ],
        scratch_shapes=[
            pltpu.VMEM((tb, GLA_WIDTH), F32),
            pltpu.VMEM(st_shape, F32),
        ],
        compiler_params=_params("arbitrary", "arbitrary", "arbitrary"),
        name="gla_ctx",
    )(qk, qk, p, p, tot, norm_g.reshape(1, GLA_WIDTH))


def _gla_latent(p, qk, tot, s0, norm_g, *, n_batch):
    _, L, _ = p.shape
    tb = GLA_TOKENS
    nblk = L // tb
    st_shape = ((GLA_HEADS // 2) * GLA_DV, 2 * GLA_DK)

    def visited(ph, j):
        return jnp.where(ph == 0, j, nblk - 1 - j)

    def fwd_only(cidx):
        return lambda b, ph, j: (b, jnp.where(ph == 0, j, nblk - 1), cidx)

    def bwd_only(cidx):
        return lambda b, ph, j: (b, jnp.where(ph == 0, nblk - 1, nblk - 1 - j), cidx)

    return pl.pallas_call(
        functools.partial(_gla_kernel, tokens=tb, nblk=nblk, is_ctx=False),
        grid=(n_batch, 2, nblk),
        in_specs=[
            pl.BlockSpec((1, tb, QK_COLS), fwd_only(0)),
            pl.BlockSpec((1, tb, QK_COLS), bwd_only(1)),
            pl.BlockSpec((1, tb, GLA_WIDTH), lambda b, ph, j: (b, visited(ph, j), 1)),
            pl.BlockSpec((1, tb, GLA_WIDTH), bwd_only(2)),
            pl.BlockSpec((1, tb // GLA_CHUNK, 2 * GLA_KW), lambda b, ph, j: (b, visited(ph, j), 0)),
            pl.BlockSpec((1, 1) + st_shape, lambda b, ph, j: (b, ph, 0, 0)),
            _resident((1, GLA_WIDTH)),
        ],
        out_specs=pl.BlockSpec((1, tb, GLA_WIDTH), bwd_only(0)),
        out_shape=jax.ShapeDtypeStruct((n_batch, L, GLA_WIDTH), BF16),
        scratch_shapes=[
            pltpu.VMEM((L, GLA_WIDTH), F32),
            pltpu.VMEM(st_shape, F32),
        ],
        compiler_params=_params("arbitrary", "arbitrary", "arbitrary"),
        name="gla_latent",
    )(qk, qk, p, p, tot, s0, norm_g.reshape(1, GLA_WIDTH))


def _mixout_kernel(h_ref, u_ref, up_ref, un_ref, glat_ref, gctx_ref, mod_ref, band_ref, pw_ref, ps_ref, w_ref,
                   g_ref, b_ref, o_ref, ue_ref, *, n_batch, alpha):
    tm = h_ref.shape[1]
    i = pl.program_id(1)
    is_ctx = pl.program_id(0) == n_batch
    first = is_ctx | (i == 0)
    last = is_ctx | (i == pl.num_programs(1) - 1)
    no_halo = jnp.zeros((HALO, POOL_WIDTH), BF16)
    ue_ref[0:HALO, :] = jnp.where(first, no_halo, up_ref[0])
    ue_ref[HALO:HALO + tm, :] = u_ref[0]
    ue_ref[HALO + tm:, :] = jnp.where(last, no_halo, un_ref[0])

    pos = lax.broadcasted_iota(jnp.int32, (POOL_TOKENS, 1), 0)
    nsub = tm // POOL_TOKENS
    ys = []
    for s in range(nsub):
        starts = first if s == 0 else is_ctx
        ends = last if s == nsub - 1 else is_ctx
        room_lo = jnp.where(starts, pos, POOL_TOKENS)
        room_hi = jnp.where(ends, POOL_TOKENS - 1 - pos, POOL_TOKENS)
        r0 = s * POOL_TOKENS
        yg = []
        for g, w in enumerate(POOL_WINDOWS):
            cs = slice(g * POOL_GROUP, (g + 1) * POOL_GROUP)
            lo, hi = w // 2, w - 1 - w // 2
            wsum = _dot(band_ref[0, g], ue_ref[r0:r0 + POOL_TOKENS + 2 * HALO, cs])
            cnt = (jnp.minimum(room_lo, lo) + jnp.minimum(room_hi, hi) + 1).astype(F32)
            pooled = wsum / cnt - ue_ref[HALO + r0:HALO + r0 + POOL_TOKENS, cs].astype(F32)
            yg.append(_dot(pooled.astype(BF16), pw_ref[g]))
        ys.append(jnp.concatenate(yg, axis=1))
    pool_y = (jnp.concatenate(ys, axis=0) * ps_ref[...]).astype(BF16)
    gla = jnp.where(is_ctx, gctx_ref[0], glat_ref[0])
    y = _dot(pool_y, w_ref[:POOL_WIDTH, :]) + _dot(gla, w_ref[POOL_WIDTH:, :])
    z = alpha * h_ref[0] + mod_ref[0, 5:6, :] * y
    o_ref[0] = _layernorm(z) * g_ref[...] + b_ref[...]


def _pool_bands():
    t = jnp.arange(POOL_TOKENS)[:, None]
    j = jnp.arange(POOL_TOKENS + 2 * HALO)[None, :] - HALO
    inside = (j >= 0) & (j < POOL_TOKENS)
    bands = []
    for w in POOL_WINDOWS:
        lo, hi = w // 2, w - 1 - w // 2
        bands.append((j >= t - lo) & (j <= t + hi))
    bands = jnp.stack(bands)
    return jnp.stack([bands, bands & inside]).astype(BF16)


def _mixout(h, p, g_lat, g_ctx, mod, bands, pool_w, pool_scale, w_out, ln_g, ln_b, *, n_batch, alpha, nseg):
    _, L, D = h.shape
    tm = FFN_TOKENS
    nt = L // tm
    hb = tm // HALO
    return pl.pallas_call(
        functools.partial(_mixout_kernel, n_batch=n_batch, alpha=alpha),
        grid=(nseg, nt),
        in_specs=[
            pl.BlockSpec((1, tm, D), lambda b, i: (b, i, 0)),
            pl.BlockSpec((1, tm, POOL_WIDTH), lambda b, i: (b, i, 0)),
            pl.BlockSpec((1, HALO, POOL_WIDTH), lambda b, i: (b, jnp.maximum(i * hb - 1, 0), 0)),
            pl.BlockSpec((1, HALO, POOL_WIDTH), lambda b, i: (b, jnp.minimum((i + 1) * hb, nt * hb - 1), 0)),
            pl.BlockSpec((1, tm, GLA_WIDTH),
                         lambda b, i: (jnp.minimum(b, n_batch - 1), jnp.where(b == n_batch, nt - 1, i), 0)),
            pl.BlockSpec((1, tm, GLA_WIDTH), lambda b, i: (0, jnp.where(b == n_batch, i, 0), 0)),
            pl.BlockSpec((1, N_MOD, D), lambda b, i: (b, 0, 0)),
            pl.BlockSpec((1,) + bands.shape[1:], lambda b, i: (jnp.where(b == n_batch, 1, 0), 0, 0, 0)),
            _resident(pool_w.shape),
            _resident((1, POOL_WIDTH)),
            _resident(w_out.shape),
            _resident((1, D)),
            _resident((1, D)),
        ],
        out_specs=pl.BlockSpec((1, tm, D), lambda b, i: (b, i, 0)),
        out_shape=jax.ShapeDtypeStruct((nseg, L, D), F32),
        scratch_shapes=[pltpu.VMEM((tm + 2 * HALO, POOL_WIDTH), BF16)],
        compiler_params=_params("arbitrary", "arbitrary"),
        name="mixout",
    )(h, p, p, p, g_lat, g_ctx, mod, bands, pool_w, pool_scale.reshape(1, POOL_WIDTH), w_out,
      ln_g.reshape(1, D), ln_b.reshape(1, D))


def _pos_embed_2d(L):
    rows = L // GRID_W
    r = jnp.repeat(jnp.arange(rows, dtype=F32), GRID_W)
    col = jnp.tile(jnp.arange(GRID_W, dtype=F32), rows)
    quarter = D_MODEL // 4
    omega = 1.0 / (10000.0 ** (jnp.arange(quarter, dtype=F32) / quarter))

    def enc(p):
        a = p[:, None] * omega
        return jnp.concatenate([jnp.sin(a), jnp.cos(a)], axis=-1)

    return jnp.concatenate([enc(r), enc(col)], axis=-1)


def kernel(x, c, ctx, c_ctx, w_ada, b_ada, ln_g, ln_b, ffa_w_in, ffa_w_out, mix_w_in, pool_w, pool_scale,
           gate_up_f, gate_bias_f, gate_up_b, gate_bias_b, gla_norm_g, mix_w_out, ffb_w_in, ffb_w_out):
    B, L, D = x.shape
    LC = ctx.shape[1]
    depth = w_ada.shape[0]
    assert D == D_MODEL and LC == SEQ_TOKENS == POOL_TOKENS and B * LC == L
    assert L % FFN_TOKENS == 0 and L % GLA_TOKENS == 0
    alpha = (2.0 * depth) ** 0.25
    nseg = B + 1

    rows = -(-nseg // 8) * 8
    cond = jnp.concatenate([c, c_ctx[None, :], jnp.zeros((rows - nseg, D), F32)], axis=0)
    mod = _ada(cond, w_ada, b_ada)[:, :nseg].reshape(depth, nseg, N_MOD, D)

    h = _prenorm(x, _pos_embed_2d(L), ctx.reshape(1, L, D))

    zeros_gd = jnp.zeros((depth, D, GD_PAD - 2 * GATE_RANK), F32)
    w_mix_in = jnp.concatenate([mix_w_in, zeros_gd], axis=-1).astype(BF16)
    gup = jnp.zeros((depth, GD_PAD, 2 * GLA_KW), F32)
    gup = gup.at[:, :GATE_RANK, :GLA_KW].set(gate_up_f)
    gup = gup.at[:, GATE_RANK:2 * GATE_RANK, GLA_KW:].set(gate_up_b).astype(BF16)
    gbias = jnp.concatenate([gate_bias_f, gate_bias_b], axis=-1).reshape(depth, 1, 2 * GLA_KW)
    ffa_in, ffa_out = ffa_w_in.astype(BF16), ffa_w_out.astype(BF16)
    ffb_in, ffb_out = ffb_w_in.astype(BF16), ffb_w_out.astype(BF16)
    w_mix_out = mix_w_out.astype(BF16)
    pw = pool_w.astype(BF16)
    ti = jnp.arange(CUM_TOKENS)
    same_chunk = (ti[:, None] // GLA_CHUNK) == (ti[None, :] // GLA_CHUNK)
    tri_f = (same_chunk & (ti[None, :] <= ti[:, None])).astype(BF16)
    tri_b = (same_chunk & (ti[None, :] >= ti[:, None])).astype(BF16)
    bands = _pool_bands()

    for l in range(depth):
        last = l == depth - 1
        h = _ffn(h, mod[l], ffa_in[l], ffa_out[l], ln_g[l, 0], ln_b[l, 0], k0=0, alpha=alpha, nseg=nseg)
        p, qk, tot = _inproj(h, mod[l], w_mix_in[l], gup[l], gbias[l], tri_f, tri_b)
        g_ctx, s_ctx = _gla_ctx(p, qk, tot, gla_norm_g[l], n_batch=B)
        g_lat = _gla_latent(p, qk, tot, s_ctx, gla_norm_g[l], n_batch=B)
        nout = B if last else nseg
        h = _mixout(h, p, g_lat, g_ctx, mod[l], bands, pw[l], pool_scale[l], w_mix_out[l], ln_g[l, 1], ln_b[l, 1],
                    n_batch=B, alpha=alpha, nseg=nout)
        h = _ffn(h, mod[l], ffb_in[l], ffb_out[l], ln_g[l, 2], ln_b[l, 2], k0=6, alpha=alpha, nseg=nout)
    return h
```

```python
import functools

import jax
import jax.numpy as jnp
from jax import lax
from jax.experimental import pallas as pl
from jax.experimental.pallas import tpu as pltpu

F32 = jnp.float32
BF16 = jnp.bfloat16

D_MODEL = 1024
D_FF = 2816
N_MOD = 9
POOL_WINDOWS = (2, 4, 8, 16)
POOL_GROUP = 128
POOL_WIDTH = POOL_GROUP * len(POOL_WINDOWS)
GLA_HEADS = 4
GLA_DK = 64
GLA_DV = 128
GLA_KW = GLA_HEADS * GLA_DK
GLA_WIDTH = GLA_HEADS * GLA_DV
GATE_RANK = 16
GATE_TAU = 16.0
GLA_CHUNK = 64
GRID_W = 64
LN_EPS = 1e-6
MAIN_COLS = POOL_WIDTH + 2 * GLA_KW + 2 * GLA_WIDTH
P_COLS = POOL_WIDTH + 2 * GLA_WIDTH
QK_COLS = 3 * GLA_KW
GD_PAD = 256

VMEM_LIMIT_BYTES = 56 * 1024 * 1024
FFN_TOKENS = 512
FFN_CHUNK = 256
FFN_ROWS = 256
SEQ_TOKENS = 256
GLA_TOKENS = 1024
SUB_TOKENS = 128
CUM_TOKENS = 256
POOL_TOKENS = 256
HALO = 16


def _dot(a, b):
    return jnp.dot(a, b, preferred_element_type=F32)


def _dot_nt(a, b):
    return lax.dot_general(a, b, (((1,), (1,)), ((), ())), preferred_element_type=F32)


def _dot_tn(a, b):
    return lax.dot_general(a, b, (((0,), (0,)), ((), ())), preferred_element_type=F32)


def _silu(x):
    return x / (1.0 + jnp.exp(-x))


def _layernorm(z):
    mu = jnp.mean(z, axis=-1, keepdims=True)
    zc = z - mu
    var = jnp.mean(zc * zc, axis=-1, keepdims=True)
    return zc * lax.rsqrt(var + LN_EPS)


def _params(*sem):
    return pltpu.CompilerParams(dimension_semantics=sem, vmem_limit_bytes=VMEM_LIMIT_BYTES)


def _resident(shape):
    nd = len(shape)
    return pl.BlockSpec(shape, lambda *_: (0,) * nd, pipeline_mode=pl.Buffered(1))


def _ada_kernel(c_ref, w_ref, b_ref, o_ref):
    s = _silu(c_ref[...])
    o_ref[0] = _dot(s.astype(BF16), w_ref[0].astype(BF16)) + b_ref[0]


def _ada(cond, w_ada, b_ada):
    depth, d, n = w_ada.shape
    rows = cond.shape[0]
    tn = 1024
    return pl.pallas_call(
        _ada_kernel,
        grid=(depth, n // tn),
        in_specs=[
            pl.BlockSpec((rows, d), lambda l, j: (0, 0)),
            pl.BlockSpec((1, d, tn), lambda l, j: (l, 0, j)),
            pl.BlockSpec((1, 1, tn), lambda l, j: (l, 0, j)),
        ],
        out_specs=pl.BlockSpec((1, rows, tn), lambda l, j: (l, 0, j)),
        out_shape=jax.ShapeDtypeStruct((depth, rows, n), F32),
        compiler_params=_params("arbitrary", "arbitrary"),
        name="ada",
    )(cond, w_ada, b_ada.reshape(depth, 1, n))


def _ffn_rows(xr, mod_ref, k0, w_in_ref, w_out_ref, a_ref, rs, g_ref, b_ref, alpha):
    shift = mod_ref[0, k0:k0 + 1, :]
    scale = mod_ref[0, k0 + 1:k0 + 2, :]
    gate = mod_ref[0, k0 + 2:k0 + 3, :]
    xm = (xr * (1.0 + scale) + shift).astype(BF16)
    for j in range(D_FF // FFN_CHUNK):
        lo = j * FFN_CHUNK
        g = _dot(xm, w_in_ref[0, :, lo:lo + FFN_CHUNK])
        u = _dot(xm, w_in_ref[0, :, D_FF + lo:D_FF + lo + FFN_CHUNK])
        a_ref[rs, lo:lo + FFN_CHUNK] = (_silu(g) * u).astype(BF16)
    y = _dot(a_ref[rs, :], w_out_ref[0])
    z = alpha * xr + (0.5 * gate) * y
    return _layernorm(z) * g_ref[0] + b_ref[0]


def _inproj_rows(hr, mod_ref, w_ref, gup_ref, gb_ref, trif_ref, trib_ref, p_ref, qk_ref, tot_ref, rs, chunk0):
    shift = mod_ref[0, 3:4, :]
    scale = mod_ref[0, 4:5, :]
    xm = (hr * (1.0 + scale) + shift).astype(BF16)
    gd = _dot(xm, w_ref[0, :, MAIN_COLS:MAIN_COLS + GD_PAD])
    z = _dot(gd.astype(BF16), gup_ref[0]) + gb_ref[0]
    la = (jnp.minimum(z, 0.0) - jnp.log1p(jnp.exp(-jnp.abs(z)))) * (1.0 / GATE_TAU)
    la_hi = la.astype(BF16)
    la_lo = (la - la_hi.astype(F32)).astype(BF16)
    cums = [_dot(tri_ref[...], la_hi[:, d * GLA_KW:(d + 1) * GLA_KW])
            + _dot(tri_ref[...], la_lo[:, d * GLA_KW:(d + 1) * GLA_KW]) for d, tri_ref in enumerate((trif_ref, trib_ref))]
    qk = _dot(xm, w_ref[0, :, POOL_WIDTH:POOL_WIDTH + 2 * GLA_KW])
    p_ref[0, rs, 0:POOL_WIDTH] = _dot(xm, w_ref[0, :, 0:POOL_WIDTH]).astype(BF16)
    for j in (1, 2):
        p_ref[0, rs, j * 512:(j + 1) * 512] = _dot(xm, w_ref[0, :, 512 + j * 512:1024 + j * 512]).astype(BF16)
    q = qk[:, :GLA_KW] * (GLA_DK ** -0.5)
    k = qk[:, GLA_KW:]
    for d, bcs in enumerate(cums):
        cs = slice(d * GLA_KW, (d + 1) * GLA_KW)
        tots = []
        for c in range(CUM_TOKENS // GLA_CHUNK):
            e = c * GLA_CHUNK + (GLA_CHUNK - 1 if d == 0 else 0)
            t = bcs[e:e + 1, :]
            tot_ref[0, chunk0 + c:chunk0 + c + 1, cs] = t
            tots.append(jnp.broadcast_to(t, (GLA_CHUNK, GLA_KW)))
        tot = jnp.concatenate(tots, axis=0)
        base = d * QK_COLS
        qk_ref[0, rs, base:base + GLA_KW] = (q * jnp.exp(bcs)).astype(BF16)
        qk_ref[0, rs, base + GLA_KW:base + 2 * GLA_KW] = (k * jnp.exp(-bcs)).astype(BF16)
        qk_ref[0, rs, base + 2 * GLA_KW:base + 3 * GLA_KW] = (k * jnp.exp(tot - bcs)).astype(BF16)


def _front_kernel(*refs, n_batch, alpha, first_layer):
    if first_layer:
        x_ref, pos_ref, ctx_ref = refs[:3]
        refs = refs[3:]
    else:
        h_ref = refs[0]
        refs = refs[1:]
    (mod_ref, w_in_ref, w_out_ref, g_ref, b_ref, wmix_ref, gup_ref, gb_ref, trif_ref, trib_ref,
     o_ref, p_ref, qk_ref, tot_ref, a_ref) = refs
    is_ctx = pl.program_id(0) == n_batch
    groups = [slice(r * FFN_ROWS, (r + 1) * FFN_ROWS) for r in range(o_ref.shape[1] // FFN_ROWS)]
    for rs in groups:
        if first_layer:
            xr = _layernorm(jnp.where(is_ctx, ctx_ref[0, rs, :], x_ref[0, rs, :] + pos_ref[rs, :]))
        else:
            xr = h_ref[0, rs, :]
        o_ref[0, rs, :] = _ffn_rows(xr, mod_ref, 0, w_in_ref, w_out_ref, a_ref, rs, g_ref, b_ref, alpha)
    for r, rs in enumerate(groups):
        _inproj_rows(o_ref[0, rs, :], mod_ref, wmix_ref, gup_ref, gb_ref, trif_ref, trib_ref, p_ref, qk_ref, tot_ref,
                     rs, r * (FFN_ROWS // GLA_CHUNK))


def _layer_block(arr, l):
    nd = arr.ndim
    return pl.BlockSpec((1,) + arr.shape[1:], lambda *_: (l,) + (0,) * (nd - 1), pipeline_mode=pl.Buffered(1))


def _front(src, mod, l, w_in, w_out, ln_g, ln_b, w_mix, gup, gbias, tri_f, tri_b, *, n_batch, alpha):
    first_layer = isinstance(src, tuple)
    B = n_batch
    nseg = B + 1
    tm = FFN_TOKENS
    if first_layer:
        x, pos, ctx_flat = src
        _, L, D = x.shape
        nt = L // tm
        src_specs = [
            pl.BlockSpec((1, tm, D), lambda b, i: (jnp.minimum(b, B - 1), jnp.where(b == B, nt - 1, i), 0)),
            pl.BlockSpec((tm, D), lambda b, i: (jnp.where(b == B, nt - 1, i), 0)),
            pl.BlockSpec((1, tm, D), lambda b, i: (0, jnp.where(b == B, i, 0), 0)),
        ]
    else:
        src = (src,)
        _, L, D = src[0].shape
        nt = L // tm
        src_specs = [pl.BlockSpec((1, tm, D), lambda b, i: (b, i, 0))]
    return pl.pallas_call(
        functools.partial(_front_kernel, n_batch=B, alpha=alpha, first_layer=first_layer),
        grid=(nseg, nt),
        in_specs=src_specs + [
            pl.BlockSpec((1, N_MOD, D), lambda b, i: (b, 0, 0)),
            _layer_block(w_in, l), _layer_block(w_out, l), _layer_block(ln_g, 3 * l), _layer_block(ln_b, 3 * l),
            _layer_block(w_mix, l), _layer_block(gup, l), _layer_block(gbias, l),
            _resident(tri_f.shape), _resident(tri_b.shape),
        ],
        out_specs=[
            pl.BlockSpec((1, tm, D), lambda b, i: (b, i, 0)),
            pl.BlockSpec((1, tm, P_COLS), lambda b, i: (b, i, 0)),
            pl.BlockSpec((1, tm, 2 * QK_COLS), lambda b, i: (b, i, 0)),
            pl.BlockSpec((1, tm // GLA_CHUNK, 2 * GLA_KW), lambda b, i: (b, i, 0)),
        ],
        out_shape=[
            jax.ShapeDtypeStruct((nseg, L, D), F32),
            jax.ShapeDtypeStruct((nseg, L, P_COLS), BF16),
            jax.ShapeDtypeStruct((nseg, L, 2 * QK_COLS), BF16),
            jax.ShapeDtypeStruct((nseg, L // GLA_CHUNK, 2 * GLA_KW), F32),
        ],
        scratch_shapes=[pltpu.VMEM((tm, D_FF), BF16)],
        compiler_params=_params("arbitrary", "arbitrary"),
        name="front",
    )(*src, mod, w_in, w_out, ln_g, ln_b, w_mix, gup, gbias, tri_f, tri_b)


def _gla_block(qk_ref, v_ref, tot_row, st_ref, *, fwd, tokens):
    pair = 2 * GLA_DK
    d = 0 if fwd else 1
    lane = lax.broadcasted_iota(jnp.int32, (pair, pair), 1)
    first_head = lane < GLA_DK
    head_mask = [jnp.where(first_head, 1.0, 0.0).astype(BF16), jnp.where(first_head, 0.0, 1.0).astype(BF16)]
    srow = lax.broadcasted_iota(jnp.int32, (2 * SUB_TOKENS, SUB_TOKENS), 0) % SUB_TOKENS
    scol = lax.broadcasted_iota(jnp.int32, (2 * SUB_TOKENS, SUB_TOKENS), 1)
    keep = ((srow // GLA_CHUNK) == (scol // GLA_CHUNK)) & ((scol <= srow) if fwd else (scol >= srow))

    nchunk = tokens // GLA_CHUNK
    st = [st_ref[p * pair:(p + 1) * pair, :] for p in range(GLA_HEADS // 2)]
    st_start = {}
    for c in (range(nchunk) if fwd else range(nchunk - 1, -1, -1)):
        rs = slice(c * GLA_CHUNK, (c + 1) * GLA_CHUNK)
        for p in range(GLA_HEADS // 2):
            st_start[c, p] = st[p].astype(BF16)
            k_end = qk_ref[0, rs, 2 * GLA_KW + p * pair:2 * GLA_KW + (p + 1) * pair]
            upd = _dot_tn(v_ref[0, rs, 2 * p * GLA_DV:(2 * p + 2) * GLA_DV], k_end)
            decay = jnp.exp(tot_row(c, slice(d * GLA_KW + p * pair, d * GLA_KW + (p + 1) * pair)))
            st[p] = st[p] * decay + jnp.where(first_head, upd[:GLA_DV], upd[GLA_DV:])
    for p in range(GLA_HEADS // 2):
        st_ref[p * pair:(p + 1) * pair, :] = st[p]

    zeros_half = jnp.zeros((GLA_CHUNK, pair), BF16)
    for sb in range(tokens // SUB_TOKENS):
        rs = slice(sb * SUB_TOKENS, (sb + 1) * SUB_TOKENS)
        o_heads = []
        for p in range(GLA_HEADS // 2):
            qp = qk_ref[0, rs, p * pair:(p + 1) * pair]
            q_h = [qp * head_mask[0], qp * head_mask[1]]
            k_in = qk_ref[0, rs, GLA_KW + p * pair:GLA_KW + (p + 1) * pair]
            sc = _dot_nt(jnp.concatenate(q_h, axis=0), k_in)
            sc = jnp.where(keep, sc, 0.0).astype(BF16)
            c0 = sb * (SUB_TOKENS // GLA_CHUNK)
            st_cat = jnp.concatenate([st_start[c0, p], st_start[c0 + 1, p]], axis=1)
            q_inter = jnp.concatenate(
                [jnp.concatenate([q_h[0][:GLA_CHUNK], zeros_half, q_h[1][:GLA_CHUNK], zeros_half], axis=0),
                 jnp.concatenate([zeros_half, q_h[0][GLA_CHUNK:], zeros_half, q_h[1][GLA_CHUNK:]], axis=0)], axis=1)
            inter = _dot_nt(q_inter, st_cat)
            for hh in range(2):
                h = 2 * p + hh
                hs = slice(hh * SUB_TOKENS, (hh + 1) * SUB_TOKENS)
                o_heads.append(_dot(sc[hs], v_ref[0, rs, h * GLA_DV:(h + 1) * GLA_DV]) + inter[hs])
        yield sb, jnp.concatenate(o_heads, axis=1)


def _gla_kernel(*refs, tokens, nblk, is_ctx):
    if is_ctx:
        qkf_ref, qkb_ref, v_ref, r_ref, tot_ref, ng_ref, o_ref, sfin_ref, of_ref, st_ref = refs
    else:
        qkf_ref, qkb_ref, v_ref, r_ref, tot_ref, s0_ref, ng_ref, o_ref, of_ref, st_ref = refs
    ph = pl.program_id(1)
    j = pl.program_id(2)
    jb = jnp.where(ph == 0, j, nblk - 1 - j)
    base = pl.multiple_of(jb * tokens, tokens)
    nchunk = tokens // GLA_CHUNK

    @pl.when(j == 0)
    def _():
        st_ref[...] = jnp.zeros_like(st_ref) if is_ctx else s0_ref[0, 0]

    if is_ctx:
        odd = (pl.program_id(0) % 2) == 1

        def tot_row(c, ls):
            return jnp.where(odd, tot_ref[0, nchunk + c:nchunk + c + 1, ls], tot_ref[0, c:c + 1, ls])
    else:
        def tot_row(c, ls):
            return tot_ref[0, c:c + 1, ls]

    @pl.when(ph == 0)
    def _():
        for sb, o in _gla_block(qkf_ref, v_ref, tot_row, st_ref, fwd=True, tokens=tokens):
            of_ref[pl.ds(base + sb * SUB_TOKENS, SUB_TOKENS), :] = o

    @pl.when(ph == 1)
    def _():
        for sb, o in _gla_block(qkb_ref, v_ref, tot_row, st_ref, fwd=False, tokens=tokens):
            rs = slice(sb * SUB_TOKENS, (sb + 1) * SUB_TOKENS)
            o = o + of_ref[pl.ds(base + sb * SUB_TOKENS, SUB_TOKENS), :]
            normed = []
            for h in range(GLA_HEADS):
                oh = o[:, h * GLA_DV:(h + 1) * GLA_DV]
                normed.append(oh * lax.rsqrt(jnp.mean(oh * oh, axis=-1, keepdims=True) + LN_EPS))
            on = jnp.concatenate(normed, axis=1) * ng_ref[...]
            o_ref[0, rs, :] = (on * _silu(r_ref[0, rs, :].astype(F32))).astype(BF16)

    if is_ctx:
        @pl.when(j == nblk - 1)
        def _():
            sfin_ref[0, 0] = st_ref[...]


def _gla_ctx(p, qk, tot, norm_g, *, n_batch):
    _, L, _ = p.shape
    tb = SEQ_TOKENS
    st_shape = ((GLA_HEADS // 2) * GLA_DV, 2 * GLA_DK)
    return pl.pallas_call(
        functools.partial(_gla_kernel, tokens=tb, nblk=1, is_ctx=True),
        grid=(n_batch, 2, 1),
        in_specs=[
            pl.BlockSpec((1, tb, QK_COLS), lambda b, ph, j: (n_batch, b, 0)),
            pl.BlockSpec((1, tb, QK_COLS), lambda b, ph, j: (n_batch, b, 1)),
            pl.BlockSpec((1, tb, GLA_WIDTH), lambda b, ph, j: (n_batch, b, 1)),
            pl.BlockSpec((1, tb, GLA_WIDTH), lambda b, ph, j: (n_batch, b, 2)),
            pl.BlockSpec((1, 2 * tb // GLA_CHUNK, 2 * GLA_KW), lambda b, ph, j: (n_batch, b // 2, 0)),
            _resident((1, GLA_WIDTH)),
        ],
        out_specs=[
            pl.BlockSpec((1, tb, GLA_WIDTH), lambda b, ph, j: (0, b, 0)),
            pl.BlockSpec((1, 1) + st_shape, lambda b, ph, j: (b, ph, 0, 0)),
        ],
        out_shape=[
            jax.ShapeDtypeStruct((1, L, GLA_WIDTH), BF16),
            jax.ShapeDtypeStruct((n_batch, 2) + st_shape, F32),
        ],
        scratch_shapes=[
            pltpu.VMEM((tb, GLA_WIDTH), F32),
            pltpu.VMEM(st_shape, F32),
        ],
        compiler_params=_params("arbitrary", "arbitrary", "arbitrary"),
        name="gla_ctx",
    )(qk, qk, p, p, tot, norm_g.reshape(1, GLA_WIDTH))


def _gla_latent(p, qk, tot, s0, norm_g, *, n_batch):
    _, L, _ = p.shape
    tb = GLA_TOKENS
    nblk = L // tb
    st_shape = ((GLA_HEADS // 2) * GLA_DV, 2 * GLA_DK)

    def visited(ph, j):
        return jnp.where(ph == 0, j, nblk - 1 - j)

    def fwd_only(cidx):
        return lambda b, ph, j: (b, jnp.where(ph == 0, j, nblk - 1), cidx)

    def bwd_only(cidx):
        return lambda b, ph, j: (b, jnp.where(ph == 0, nblk - 1, nblk - 1 - j), cidx)

    return pl.pallas_call(
        functools.partial(_gla_kernel, tokens=tb, nblk=nblk, is_ctx=False),
        grid=(n_batch, 2, nblk),
        in_specs=[
            pl.BlockSpec((1, tb, QK_COLS), fwd_only(0)),
            pl.BlockSpec((1, tb, QK_COLS), bwd_only(1)),
            pl.BlockSpec((1, tb, GLA_WIDTH), lambda b, ph, j: (b, visited(ph, j), 1)),
            pl.BlockSpec((1, tb, GLA_WIDTH), bwd_only(2)),
            pl.BlockSpec((1, tb // GLA_CHUNK, 2 * GLA_KW), lambda b, ph, j: (b, visited(ph, j), 0)),
            pl.BlockSpec((1, 1) + st_shape, lambda b, ph, j: (b, ph, 0, 0)),
            _resident((1, GLA_WIDTH)),
        ],
        out_specs=pl.BlockSpec((1, tb, GLA_WIDTH), bwd_only(0)),
        out_shape=jax.ShapeDtypeStruct((n_batch, L, GLA_WIDTH), BF16),
        scratch_shapes=[
            pltpu.VMEM((L, GLA_WIDTH), F32),
            pltpu.VMEM(st_shape, F32),
        ],
        compiler_params=_params("arbitrary", "arbitrary", "arbitrary"),
        name="gla_latent",
    )(qk, qk, p, p, tot, s0, norm_g.reshape(1, GLA_WIDTH))


def _back_kernel(h_ref, u_ref, up_ref, un_ref, glat_ref, gctx_ref, mod_ref, band_ref, pw_ref, ps_ref, wmix_ref,
                 g1_ref, b1_ref, w_in_ref, w_out_ref, g2_ref, b2_ref, o_ref, ue_ref, a_ref, *, n_batch, alpha):
    tm = h_ref.shape[1]
    i = pl.program_id(1)
    is_ctx = pl.program_id(0) == n_batch
    first = is_ctx | (i == 0)
    last = is_ctx | (i == pl.num_programs(1) - 1)
    no_halo = jnp.zeros((HALO, POOL_WIDTH), BF16)
    ue_ref[0:HALO, :] = jnp.where(first, no_halo, up_ref[0])
    ue_ref[HALO:HALO + tm, :] = u_ref[0]
    ue_ref[HALO + tm:, :] = jnp.where(last, no_halo, un_ref[0])

    pos = lax.broadcasted_iota(jnp.int32, (POOL_TOKENS, 1), 0)
    nsub = tm // POOL_TOKENS
    for s in range(nsub):
        rs = slice(s * POOL_TOKENS, (s + 1) * POOL_TOKENS)
        starts = first if s == 0 else is_ctx
        ends = last if s == nsub - 1 else is_ctx
        room_lo = jnp.where(starts, pos, POOL_TOKENS)
        room_hi = jnp.where(ends, POOL_TOKENS - 1 - pos, POOL_TOKENS)
        r0 = s * POOL_TOKENS
        yg = []
        for g, w in enumerate(POOL_WINDOWS):
            cs = slice(g * POOL_GROUP, (g + 1) * POOL_GROUP)
            lo, hi = w // 2, w - 1 - w // 2
            wsum = _dot(band_ref[0, g], ue_ref[r0:r0 + POOL_TOKENS + 2 * HALO, cs])
            cnt = (jnp.minimum(room_lo, lo) + jnp.minimum(room_hi, hi) + 1).astype(F32)
            pooled = wsum / cnt - ue_ref[HALO + r0:HALO + r0 + POOL_TOKENS, cs].astype(F32)
            yg.append(_dot(pooled.astype(BF16), pw_ref[0, g]))
        pool_y = (jnp.concatenate(yg, axis=1) * ps_ref[0]).astype(BF16)
        gla = jnp.where(is_ctx, gctx_ref[0, rs, :], glat_ref[0, rs, :])
        y = _dot(pool_y, wmix_ref[0, :POOL_WIDTH, :]) + _dot(gla, wmix_ref[0, POOL_WIDTH:, :])
        z = alpha * h_ref[0, rs, :] + mod_ref[0, 5:6, :] * y
        o_ref[0, rs, :] = _layernorm(z) * g1_ref[0] + b1_ref[0]
    for s in range(nsub):
        rs = slice(s * POOL_TOKENS, (s + 1) * POOL_TOKENS)
        o_ref[0, rs, :] = _ffn_rows(o_ref[0, rs, :], mod_ref, 6, w_in_ref, w_out_ref, a_ref, rs, g2_ref, b2_ref,
                                    alpha)


def _pool_bands():
    t = jnp.arange(POOL_TOKENS)[:, None]
    j = jnp.arange(POOL_TOKENS + 2 * HALO)[None, :] - HALO
    inside = (j >= 0) & (j < POOL_TOKENS)
    bands = []
    for w in POOL_WINDOWS:
        lo, hi = w // 2, w - 1 - w // 2
        bands.append((j >= t - lo) & (j <= t + hi))
    bands = jnp.stack(bands)
    return jnp.stack([bands, bands & inside]).astype(BF16)


def _back(h, p, g_lat, g_ctx, mod, l, bands, pool_w, pool_scale, w_mix_out, ln_g, ln_b, w_in, w_out,
          *, n_batch, alpha, nseg):
    _, L, D = h.shape
    tm = FFN_TOKENS
    nt = L // tm
    hb = tm // HALO
    return pl.pallas_call(
        functools.partial(_back_kernel, n_batch=n_batch, alpha=alpha),
        grid=(nseg, nt),
        in_specs=[
            pl.BlockSpec((1, tm, D), lambda b, i: (b, i, 0)),
            pl.BlockSpec((1, tm, POOL_WIDTH), lambda b, i: (b, i, 0)),
            pl.BlockSpec((1, HALO, POOL_WIDTH), lambda b, i: (b, jnp.maximum(i * hb - 1, 0), 0)),
            pl.BlockSpec((1, HALO, POOL_WIDTH), lambda b, i: (b, jnp.minimum((i + 1) * hb, nt * hb - 1), 0)),
            pl.BlockSpec((1, tm, GLA_WIDTH),
                         lambda b, i: (jnp.minimum(b, n_batch - 1), jnp.where(b == n_batch, nt - 1, i), 0)),
            pl.BlockSpec((1, tm, GLA_WIDTH), lambda b, i: (0, jnp.where(b == n_batch, i, 0), 0)),
            pl.BlockSpec((1, N_MOD, D), lambda b, i: (b, 0, 0)),
            pl.BlockSpec((1,) + bands.shape[1:], lambda b, i: (jnp.where(b == n_batch, 1, 0), 0, 0, 0)),
            _layer_block(pool_w, l), _layer_block(pool_scale, l), _layer_block(w_mix_out, l),
            _layer_block(ln_g, 3 * l + 1), _layer_block(ln_b, 3 * l + 1),
            _layer_block(w_in, l), _layer_block(w_out, l),
            _layer_block(ln_g, 3 * l + 2), _layer_block(ln_b, 3 * l + 2),
        ],
        out_specs=pl.BlockSpec((1, tm, D), lambda b, i: (b, i, 0)),
        out_shape=jax.ShapeDtypeStruct((nseg, L, D), F32),
        scratch_shapes=[pltpu.VMEM((tm + 2 * HALO, POOL_WIDTH), BF16), pltpu.VMEM((tm, D_FF), BF16)],
        compiler_params=_params("arbitrary", "arbitrary"),
        name="back",
    )(h, p, p, p, g_lat, g_ctx, mod, bands, pool_w, pool_scale, w_mix_out, ln_g, ln_b, w_in, w_out, ln_g, ln_b)


def _pos_embed_2d(L):
    rows = L // GRID_W
    r = jnp.repeat(jnp.arange(rows, dtype=F32), GRID_W)
    col = jnp.tile(jnp.arange(GRID_W, dtype=F32), rows)
    quarter = D_MODEL // 4
    omega = 1.0 / (10000.0 ** (jnp.arange(quarter, dtype=F32) / quarter))

    def enc(p):
        a = p[:, None] * omega
        return jnp.concatenate([jnp.sin(a), jnp.cos(a)], axis=-1)

    return jnp.concatenate([enc(r), enc(col)], axis=-1)


def kernel(x, c, ctx, c_ctx, w_ada, b_ada, ln_g, ln_b, ffa_w_in, ffa_w_out, mix_w_in, pool_w, pool_scale,
           gate_up_f, gate_bias_f, gate_up_b, gate_bias_b, gla_norm_g, mix_w_out, ffb_w_in, ffb_w_out):
    B, L, D = x.shape
    LC = ctx.shape[1]
    depth = w_ada.shape[0]
    assert D == D_MODEL and LC == SEQ_TOKENS == POOL_TOKENS == FFN_ROWS == CUM_TOKENS and B * LC == L
    assert L % FFN_TOKENS == 0 and L % GLA_TOKENS == 0
    alpha = (2.0 * depth) ** 0.25
    nseg = B + 1

    rows = -(-nseg // 8) * 8
    cond = jnp.concatenate([c, c_ctx[None, :], jnp.zeros((rows - nseg, D), F32)], axis=0)
    mod = _ada(cond, w_ada, b_ada)[:, :nseg].reshape(depth, nseg, N_MOD, D)

    zeros_gd = jnp.zeros((depth, D, GD_PAD - 2 * GATE_RANK), F32)
    w_mix_in = jnp.concatenate([mix_w_in, zeros_gd], axis=-1).astype(BF16)
    gup = jnp.zeros((depth, GD_PAD, 2 * GLA_KW), F32)
    gup = gup.at[:, :GATE_RANK, :GLA_KW].set(gate_up_f)
    gup = gup.at[:, GATE_RANK:2 * GATE_RANK, GLA_KW:].set(gate_up_b).astype(BF16)
    gbias = jnp.concatenate([gate_bias_f, gate_bias_b], axis=-1).reshape(depth, 1, 2 * GLA_KW)
    ffa_in, ffa_out = ffa_w_in.astype(BF16), ffa_w_out.astype(BF16)
    ffb_in, ffb_out = ffb_w_in.astype(BF16), ffb_w_out.astype(BF16)
    w_mix_out = mix_w_out.astype(BF16)
    pw = pool_w.astype(BF16)
    ps = pool_scale.reshape(depth, 1, POOL_WIDTH)
    lng = ln_g.reshape(depth * 3, 1, D)
    lnb = ln_b.reshape(depth * 3, 1, D)
    ti = jnp.arange(CUM_TOKENS)
    same_chunk = (ti[:, None] // GLA_CHUNK) == (ti[None, :] // GLA_CHUNK)
    tri_f = (same_chunk & (ti[None, :] <= ti[:, None])).astype(BF16)
    tri_b = (same_chunk & (ti[None, :] >= ti[:, None])).astype(BF16)
    bands = _pool_bands()

    h = (x, _pos_embed_2d(L), ctx.reshape(1, L, D))
    for l in range(depth):
        last = l == depth - 1
        h, p, qk, tot = _front(h, mod[l], l, ffa_in, ffa_out, lng, lnb, w_mix_in, gup, gbias, tri_f, tri_b,
                               n_batch=B, alpha=alpha)
        g_ctx, s_ctx = _gla_ctx(p, qk, tot, gla_norm_g[l], n_batch=B)
        g_lat = _gla_latent(p, qk, tot, s_ctx, gla_norm_g[l], n_batch=B)
        h = _back(h, p, g_lat, g_ctx, mod[l], l, bands, pw, ps, w_mix_out, lng, lnb, ffb_in, ffb_out,
                  n_batch=B, alpha=alpha, nseg=B if last else nseg)
    return h
```

```python
import functools

import jax
import jax.numpy as jnp
from jax import lax
from jax.experimental import pallas as pl
from jax.experimental.pallas import tpu as pltpu

F32 = jnp.float32
BF16 = jnp.bfloat16

D_MODEL = 1024
D_FF = 2816
N_MOD = 9
POOL_WINDOWS = (2, 4, 8, 16)
POOL_GROUP = 128
POOL_WIDTH = POOL_GROUP * len(POOL_WINDOWS)
GLA_HEADS = 4
GLA_DK = 64
GLA_DV = 128
GLA_KW = GLA_HEADS * GLA_DK
GLA_WIDTH = GLA_HEADS * GLA_DV
GATE_RANK = 16
GATE_TAU = 16.0
GLA_CHUNK = 64
GRID_W = 64
LN_EPS = 1e-6
MAIN_COLS = POOL_WIDTH + 2 * GLA_KW + 2 * GLA_WIDTH
P_COLS = POOL_WIDTH + 2 * GLA_WIDTH
QK_COLS = 3 * GLA_KW
GD_PAD = 256

VMEM_LIMIT_BYTES = 56 * 1024 * 1024
FFN_TOKENS = 512
BACK_TOKENS = 1024
FFN_CHUNK = 256
FFN_ROWS = 256
SEQ_TOKENS = 256
GLA_TOKENS = 1024
SUB_TOKENS = 128
CUM_TOKENS = 256
POOL_TOKENS = 256
HALO = 16


def _dot(a, b):
    return jnp.dot(a, b, preferred_element_type=F32)


def _dot_nt(a, b):
    return lax.dot_general(a, b, (((1,), (1,)), ((), ())), preferred_element_type=F32)


def _dot_tn(a, b):
    return lax.dot_general(a, b, (((0,), (0,)), ((), ())), preferred_element_type=F32)


def _silu(x):
    return x / (1.0 + jnp.exp(-x))


def _layernorm(z):
    mu = jnp.mean(z, axis=-1, keepdims=True)
    zc = z - mu
    var = jnp.mean(zc * zc, axis=-1, keepdims=True)
    return zc * lax.rsqrt(var + LN_EPS)


def _params(*sem):
    return pltpu.CompilerParams(dimension_semantics=sem, vmem_limit_bytes=VMEM_LIMIT_BYTES)


def _resident(shape):
    nd = len(shape)
    return pl.BlockSpec(shape, lambda *_: (0,) * nd, pipeline_mode=pl.Buffered(1))


def _ada_kernel(c_ref, w_ref, b_ref, o_ref):
    s = _silu(c_ref[...])
    o_ref[0] = _dot(s.astype(BF16), w_ref[0].astype(BF16)) + b_ref[0]


def _ada(cond, w_ada, b_ada):
    depth, d, n = w_ada.shape
    rows = cond.shape[0]
    tn = 1024
    return pl.pallas_call(
        _ada_kernel,
        grid=(depth, n // tn),
        in_specs=[
            pl.BlockSpec((rows, d), lambda l, j: (0, 0)),
            pl.BlockSpec((1, d, tn), lambda l, j: (l, 0, j)),
            pl.BlockSpec((1, 1, tn), lambda l, j: (l, 0, j)),
        ],
        out_specs=pl.BlockSpec((1, rows, tn), lambda l, j: (l, 0, j)),
        out_shape=jax.ShapeDtypeStruct((depth, rows, n), F32),
        compiler_params=_params("arbitrary", "arbitrary"),
        name="ada",
    )(cond, w_ada, b_ada.reshape(depth, 1, n))


def _ffn_rows(xr, mod_ref, k0, w_in_ref, w_out_ref, a_ref, rs, g_ref, b_ref, alpha):
    shift = mod_ref[0, k0:k0 + 1, :]
    scale = mod_ref[0, k0 + 1:k0 + 2, :]
    gate = mod_ref[0, k0 + 2:k0 + 3, :]
    xm = (xr * (1.0 + scale) + shift).astype(BF16)
    for j in range(D_FF // FFN_CHUNK):
        lo = j * FFN_CHUNK
        g = _dot(xm, w_in_ref[0, :, lo:lo + FFN_CHUNK])
        u = _dot(xm, w_in_ref[0, :, D_FF + lo:D_FF + lo + FFN_CHUNK])
        a_ref[rs, lo:lo + FFN_CHUNK] = (_silu(g) * u).astype(BF16)
    y = _dot(a_ref[rs, :], w_out_ref[0])
    z = alpha * xr + (0.5 * gate) * y
    return _layernorm(z) * g_ref[0] + b_ref[0]


def _inproj_groups(h_ref, groups, mod_ref, w_ref, gup_ref, gb_ref, trif_ref, trib_ref, p_ref, qk_ref, tot_ref):
    shift = mod_ref[0, 3:4, :]
    scale = mod_ref[0, 4:5, :]
    xms = [(h_ref[0, rs, :] * (1.0 + scale) + shift).astype(BF16) for rs in groups]
    gds = [_dot(xm, w_ref[0, :, MAIN_COLS:MAIN_COLS + GD_PAD]) for xm in xms]
    zs = [_dot(gd.astype(BF16), gup_ref[0]) + gb_ref[0] for gd in gds]
    cums = []
    for z in zs:
        la = (jnp.minimum(z, 0.0) - jnp.log1p(jnp.exp(-jnp.abs(z)))) * (1.0 / GATE_TAU)
        la_hi = la.astype(BF16)
        la_lo = (la - la_hi.astype(F32)).astype(BF16)
        cums.append([_dot(tri_ref[...], la_hi[:, d * GLA_KW:(d + 1) * GLA_KW])
                     + _dot(tri_ref[...], la_lo[:, d * GLA_KW:(d + 1) * GLA_KW])
                     for d, tri_ref in enumerate((trif_ref, trib_ref))])
    qks = [_dot(xm, w_ref[0, :, POOL_WIDTH:POOL_WIDTH + 2 * GLA_KW]) for xm in xms]
    for rs, xm in zip(groups, xms):
        p_ref[0, rs, 0:POOL_WIDTH] = _dot(xm, w_ref[0, :, 0:POOL_WIDTH]).astype(BF16)
        for j in (1, 2):
            p_ref[0, rs, j * 512:(j + 1) * 512] = _dot(xm, w_ref[0, :, 512 + j * 512:1024 + j * 512]).astype(BF16)
    for r, (rs, qk) in enumerate(zip(groups, qks)):
        q = qk[:, :GLA_KW] * (GLA_DK ** -0.5)
        k = qk[:, GLA_KW:]
        for d, bcs in enumerate(cums[r]):
            cs = slice(d * GLA_KW, (d + 1) * GLA_KW)
            tots = []
            for c in range(CUM_TOKENS // GLA_CHUNK):
                e = c * GLA_CHUNK + (GLA_CHUNK - 1 if d == 0 else 0)
                t = bcs[e:e + 1, :]
                row = r * (CUM_TOKENS // GLA_CHUNK) + c
                tot_ref[0, row:row + 1, cs] = t
                tots.append(jnp.broadcast_to(t, (GLA_CHUNK, GLA_KW)))
            tot = jnp.concatenate(tots, axis=0)
            base = d * QK_COLS
            qk_ref[0, rs, base:base + GLA_KW] = (q * jnp.exp(bcs)).astype(BF16)
            qk_ref[0, rs, base + GLA_KW:base + 2 * GLA_KW] = (k * jnp.exp(-bcs)).astype(BF16)
            qk_ref[0, rs, base + 2 * GLA_KW:base + 3 * GLA_KW] = (k * jnp.exp(tot - bcs)).astype(BF16)


def _front_kernel(*refs, n_batch, alpha, first_layer):
    if first_layer:
        x_ref, pos_ref, ctx_ref = refs[:3]
        refs = refs[3:]
    else:
        h_ref = refs[0]
        refs = refs[1:]
    (mod_ref, w_in_ref, w_out_ref, g_ref, b_ref, wmix_ref, gup_ref, gb_ref, trif_ref, trib_ref,
     o_ref, p_ref, qk_ref, tot_ref, a_ref) = refs
    is_ctx = pl.program_id(0) == n_batch
    groups = [slice(r * FFN_ROWS, (r + 1) * FFN_ROWS) for r in range(o_ref.shape[1] // FFN_ROWS)]
    for rs in groups:
        if first_layer:
            xr = _layernorm(jnp.where(is_ctx, ctx_ref[0, rs, :], x_ref[0, rs, :] + pos_ref[rs, :]))
        else:
            xr = h_ref[0, rs, :]
        o_ref[0, rs, :] = _ffn_rows(xr, mod_ref, 0, w_in_ref, w_out_ref, a_ref, rs, g_ref, b_ref, alpha)
    _inproj_groups(o_ref, groups, mod_ref, wmix_ref, gup_ref, gb_ref, trif_ref, trib_ref, p_ref, qk_ref, tot_ref)


def _layer_block(arr, l):
    nd = arr.ndim
    return pl.BlockSpec((1,) + arr.shape[1:], lambda *_: (l,) + (0,) * (nd - 1), pipeline_mode=pl.Buffered(1))


def _front(src, mod, l, w_in, w_out, ln_g, ln_b, w_mix, gup, gbias, tri_f, tri_b, *, n_batch, alpha):
    first_layer = isinstance(src, tuple)
    B = n_batch
    nseg = B + 1
    tm = FFN_TOKENS
    if first_layer:
        x, pos, ctx_flat = src
        _, L, D = x.shape
        nt = L // tm
        src_specs = [
            pl.BlockSpec((1, tm, D), lambda b, i: (jnp.minimum(b, B - 1), jnp.where(b == B, nt - 1, i), 0)),
            pl.BlockSpec((tm, D), lambda b, i: (jnp.where(b == B, nt - 1, i), 0)),
            pl.BlockSpec((1, tm, D), lambda b, i: (0, jnp.where(b == B, i, 0), 0)),
        ]
    else:
        src = (src,)
        _, L, D = src[0].shape
        nt = L // tm
        src_specs = [pl.BlockSpec((1, tm, D), lambda b, i: (b, i, 0))]
    return pl.pallas_call(
        functools.partial(_front_kernel, n_batch=B, alpha=alpha, first_layer=first_layer),
        grid=(nseg, nt),
        in_specs=src_specs + [
            pl.BlockSpec((1, N_MOD, D), lambda b, i: (b, 0, 0)),
            _layer_block(w_in, l), _layer_block(w_out, l), _layer_block(ln_g, 3 * l), _layer_block(ln_b, 3 * l),
            _layer_block(w_mix, l), _layer_block(gup, l), _layer_block(gbias, l),
            _resident(tri_f.shape), _resident(tri_b.shape),
        ],
        out_specs=[
            pl.BlockSpec((1, tm, D), lambda b, i: (b, i, 0)),
            pl.BlockSpec((1, tm, P_COLS), lambda b, i: (b, i, 0)),
            pl.BlockSpec((1, tm, 2 * QK_COLS), lambda b, i: (b, i, 0)),
            pl.BlockSpec((1, tm // GLA_CHUNK, 2 * GLA_KW), lambda b, i: (b, i, 0)),
        ],
        out_shape=[
            jax.ShapeDtypeStruct((nseg, L, D), F32),
            jax.ShapeDtypeStruct((nseg, L, P_COLS), BF16),
            jax.ShapeDtypeStruct((nseg, L, 2 * QK_COLS), BF16),
            jax.ShapeDtypeStruct((nseg, L // GLA_CHUNK, 2 * GLA_KW), F32),
        ],
        scratch_shapes=[pltpu.VMEM((tm, D_FF), BF16)],
        compiler_params=_params("arbitrary", "arbitrary"),
        name="front",
    )(*src, mod, w_in, w_out, ln_g, ln_b, w_mix, gup, gbias, tri_f, tri_b)


def _gla_block(qk_ref, v_ref, tot_row, st_ref, *, fwd, tokens):
    pair = 2 * GLA_DK
    d = 0 if fwd else 1
    lane = lax.broadcasted_iota(jnp.int32, (pair, pair), 1)
    first_head = lane < GLA_DK
    head_mask = [jnp.where(first_head, 1.0, 0.0).astype(BF16), jnp.where(first_head, 0.0, 1.0).astype(BF16)]
    srow = lax.broadcasted_iota(jnp.int32, (2 * SUB_TOKENS, SUB_TOKENS), 0) % SUB_TOKENS
    scol = lax.broadcasted_iota(jnp.int32, (2 * SUB_TOKENS, SUB_TOKENS), 1)
    keep = ((srow // GLA_CHUNK) == (scol // GLA_CHUNK)) & ((scol <= srow) if fwd else (scol >= srow))

    nchunk = tokens // GLA_CHUNK
    st = [st_ref[p * pair:(p + 1) * pair, :] for p in range(GLA_HEADS // 2)]
    st_start = {}
    for c in (range(nchunk) if fwd else range(nchunk - 1, -1, -1)):
        rs = slice(c * GLA_CHUNK, (c + 1) * GLA_CHUNK)
        for p in range(GLA_HEADS // 2):
            st_start[c, p] = st[p].astype(BF16)
            k_end = qk_ref[0, rs, 2 * GLA_KW + p * pair:2 * GLA_KW + (p + 1) * pair]
            upd = _dot_tn(v_ref[0, rs, 2 * p * GLA_DV:(2 * p + 2) * GLA_DV], k_end)
            decay = jnp.exp(tot_row(c, slice(d * GLA_KW + p * pair, d * GLA_KW + (p + 1) * pair)))
            st[p] = st[p] * decay + jnp.where(first_head, upd[:GLA_DV], upd[GLA_DV:])
    for p in range(GLA_HEADS // 2):
        st_ref[p * pair:(p + 1) * pair, :] = st[p]

    zeros_half = jnp.zeros((GLA_CHUNK, pair), BF16)
    for sb in range(tokens // SUB_TOKENS):
        rs = slice(sb * SUB_TOKENS, (sb + 1) * SUB_TOKENS)
        o_heads = []
        for p in range(GLA_HEADS // 2):
            qp = qk_ref[0, rs, p * pair:(p + 1) * pair]
            q_h = [qp * head_mask[0], qp * head_mask[1]]
            k_in = qk_ref[0, rs, GLA_KW + p * pair:GLA_KW + (p + 1) * pair]
            sc = _dot_nt(jnp.concatenate(q_h, axis=0), k_in)
            sc = jnp.where(keep, sc, 0.0).astype(BF16)
            c0 = sb * (SUB_TOKENS // GLA_CHUNK)
            st_cat = jnp.concatenate([st_start[c0, p], st_start[c0 + 1, p]], axis=1)
            q_inter = jnp.concatenate(
                [jnp.concatenate([q_h[0][:GLA_CHUNK], zeros_half, q_h[1][:GLA_CHUNK], zeros_half], axis=0),
                 jnp.concatenate([zeros_half, q_h[0][GLA_CHUNK:], zeros_half, q_h[1][GLA_CHUNK:]], axis=0)], axis=1)
            inter = _dot_nt(q_inter, st_cat)
            for hh in range(2):
                h = 2 * p + hh
                hs = slice(hh * SUB_TOKENS, (hh + 1) * SUB_TOKENS)
                o_heads.append(_dot(sc[hs], v_ref[0, rs, h * GLA_DV:(h + 1) * GLA_DV]) + inter[hs])
        yield sb, jnp.concatenate(o_heads, axis=1)


def _gla_kernel(*refs, tokens, nblk, is_ctx):
    if is_ctx:
        qkf_ref, qkb_ref, v_ref, r_ref, tot_ref, ng_ref, o_ref, sfin_ref, of_ref, st_ref = refs
    else:
        qkf_ref, qkb_ref, v_ref, r_ref, tot_ref, s0_ref, ng_ref, o_ref, of_ref, st_ref = refs
    ph = pl.program_id(1)
    j = pl.program_id(2)
    jb = jnp.where(ph == 0, j, nblk - 1 - j)
    base = pl.multiple_of(jb * tokens, tokens)
    nchunk = tokens // GLA_CHUNK

    @pl.when(j == 0)
    def _():
        st_ref[...] = jnp.zeros_like(st_ref) if is_ctx else s0_ref[0, 0]

    if is_ctx:
        odd = (pl.program_id(0) % 2) == 1

        def tot_row(c, ls):
            return jnp.where(odd, tot_ref[0, nchunk + c:nchunk + c + 1, ls], tot_ref[0, c:c + 1, ls])
    else:
        def tot_row(c, ls):
            return tot_ref[0, c:c + 1, ls]

    @pl.when(ph == 0)
    def _():
        for sb, o in _gla_block(qkf_ref, v_ref, tot_row, st_ref, fwd=True, tokens=tokens):
            of_ref[pl.ds(base + sb * SUB_TOKENS, SUB_TOKENS), :] = o

    @pl.when(ph == 1)
    def _():
        for sb, o in _gla_block(qkb_ref, v_ref, tot_row, st_ref, fwd=False, tokens=tokens):
            rs = slice(sb * SUB_TOKENS, (sb + 1) * SUB_TOKENS)
            o = o + of_ref[pl.ds(base + sb * SUB_TOKENS, SUB_TOKENS), :]
            normed = []
            for h in range(GLA_HEADS):
                oh = o[:, h * GLA_DV:(h + 1) * GLA_DV]
                normed.append(oh * lax.rsqrt(jnp.mean(oh * oh, axis=-1, keepdims=True) + LN_EPS))
            on = jnp.concatenate(normed, axis=1) * ng_ref[...]
            o_ref[0, rs, :] = (on * _silu(r_ref[0, rs, :].astype(F32))).astype(BF16)

    if is_ctx:
        @pl.when(j == nblk - 1)
        def _():
            sfin_ref[0, 0] = st_ref[...]


def _gla_ctx(p, qk, tot, norm_g, *, n_batch):
    _, L, _ = p.shape
    tb = SEQ_TOKENS
    st_shape = ((GLA_HEADS // 2) * GLA_DV, 2 * GLA_DK)
    return pl.pallas_call(
        functools.partial(_gla_kernel, tokens=tb, nblk=1, is_ctx=True),
        grid=(n_batch, 2, 1),
        in_specs=[
            pl.BlockSpec((1, tb, QK_COLS), lambda b, ph, j: (n_batch, b, 0)),
            pl.BlockSpec((1, tb, QK_COLS), lambda b, ph, j: (n_batch, b, 1)),
            pl.BlockSpec((1, tb, GLA_WIDTH), lambda b, ph, j: (n_batch, b, 1)),
            pl.BlockSpec((1, tb, GLA_WIDTH), lambda b, ph, j: (n_batch, b, 2)),
            pl.BlockSpec((1, 2 * tb // GLA_CHUNK, 2 * GLA_KW), lambda b, ph, j: (n_batch, b // 2, 0)),
            _resident((1, GLA_WIDTH)),
        ],
        out_specs=[
            pl.BlockSpec((1, tb, GLA_WIDTH), lambda b, ph, j: (0, b, 0)),
            pl.BlockSpec((1, 1) + st_shape, lambda b, ph, j: (b, ph, 0, 0)),
        ],
        out_shape=[
            jax.ShapeDtypeStruct((1, L, GLA_WIDTH), BF16),
            jax.ShapeDtypeStruct((n_batch, 2) + st_shape, F32),
        ],
        scratch_shapes=[
            pltpu.VMEM((tb, GLA_WIDTH), F32),
            pltpu.VMEM(st_shape, F32),
        ],
        compiler_params=_params("arbitrary", "arbitrary", "arbitrary"),
        name="gla_ctx",
    )(qk, qk, p, p, tot, norm_g.reshape(1, GLA_WIDTH))


def _gla_latent(p, qk, tot, s0, norm_g, *, n_batch):
    _, L, _ = p.shape
    tb = GLA_TOKENS
    nblk = L // tb
    st_shape = ((GLA_HEADS // 2) * GLA_DV, 2 * GLA_DK)

    def visited(ph, j):
        return jnp.where(ph == 0, j, nblk - 1 - j)

    def fwd_only(cidx):
        return lambda b, ph, j: (b, jnp.where(ph == 0, j, nblk - 1), cidx)

    def bwd_only(cidx):
        return lambda b, ph, j: (b, jnp.where(ph == 0, nblk - 1, nblk - 1 - j), cidx)

    return pl.pallas_call(
        functools.partial(_gla_kernel, tokens=tb, nblk=nblk, is_ctx=False),
        grid=(n_batch, 2, nblk),
        in_specs=[
            pl.BlockSpec((1, tb, QK_COLS), fwd_only(0)),
            pl.BlockSpec((1, tb, QK_COLS), bwd_only(1)),
            pl.BlockSpec((1, tb, GLA_WIDTH), lambda b, ph, j: (b, visited(ph, j), 1)),
            pl.BlockSpec((1, tb, GLA_WIDTH), bwd_only(2)),
            pl.BlockSpec((1, tb // GLA_CHUNK, 2 * GLA_KW), lambda b, ph, j: (b, visited(ph, j), 0)),
            pl.BlockSpec((1, 1) + st_shape, lambda b, ph, j: (b, ph, 0, 0)),
            _resident((1, GLA_WIDTH)),
        ],
        out_specs=pl.BlockSpec((1, tb, GLA_WIDTH), bwd_only(0)),
        out_shape=jax.ShapeDtypeStruct((n_batch, L, GLA_WIDTH), BF16),
        scratch_shapes=[
            pltpu.VMEM((L, GLA_WIDTH), F32),
            pltpu.VMEM(st_shape, F32),
        ],
        compiler_params=_params("arbitrary", "arbitrary", "arbitrary"),
        name="gla_latent",
    )(qk, qk, p, p, tot, s0, norm_g.reshape(1, GLA_WIDTH))


def _back_kernel(h_ref, u_ref, up_ref, un_ref, glat_ref, gctx_ref, mod_ref, band_ref, pw_ref, ps_ref, wmix_ref,
                 g1_ref, b1_ref, w_in_ref, w_out_ref, g2_ref, b2_ref, o_ref, ue_ref, a_ref, *, n_batch, alpha):
    tm = h_ref.shape[1]
    i = pl.program_id(1)
    is_ctx = pl.program_id(0) == n_batch
    first = is_ctx | (i == 0)
    last = is_ctx | (i == pl.num_programs(1) - 1)
    no_halo = jnp.zeros((HALO, POOL_WIDTH), BF16)
    ue_ref[0:HALO, :] = jnp.where(first, no_halo, up_ref[0])
    ue_ref[HALO:HALO + tm, :] = u_ref[0]
    ue_ref[HALO + tm:, :] = jnp.where(last, no_halo, un_ref[0])

    pos = lax.broadcasted_iota(jnp.int32, (POOL_TOKENS, 1), 0)
    nsub = tm // POOL_TOKENS

    groups = [slice(s * POOL_TOKENS, (s + 1) * POOL_TOKENS) for s in range(nsub)]
    wsums = [[_dot(band_ref[0, g], ue_ref[rs.start:rs.stop + 2 * HALO, g * POOL_GROUP:(g + 1) * POOL_GROUP])
              for g in range(len(POOL_WINDOWS))] for rs in groups]
    pool_ys = []
    for s, rs in enumerate(groups):
        starts = first if s == 0 else is_ctx
        ends = last if s == nsub - 1 else is_ctx
        room_lo = jnp.where(starts, pos, POOL_TOKENS)
        room_hi = jnp.where(ends, POOL_TOKENS - 1 - pos, POOL_TOKENS)
        yg = []
        for g, w in enumerate(POOL_WINDOWS):
            cs = slice(g * POOL_GROUP, (g + 1) * POOL_GROUP)
            lo, hi = w // 2, w - 1 - w // 2
            cnt = (jnp.minimum(room_lo, lo) + jnp.minimum(room_hi, hi) + 1).astype(F32)
            pooled = wsums[s][g] / cnt - ue_ref[HALO + rs.start:HALO + rs.stop, cs].astype(F32)
            yg.append(_dot(pooled.astype(BF16), pw_ref[0, g]))
        pool_ys.append((jnp.concatenate(yg, axis=1) * ps_ref[0]).astype(BF16))
    ys = []
    for rs, pool_y in zip(groups, pool_ys):
        gla = jnp.where(is_ctx, gctx_ref[0, rs, :], glat_ref[0, rs, :])
        ys.append(_dot(pool_y, wmix_ref[0, :POOL_WIDTH, :]) + _dot(gla, wmix_ref[0, POOL_WIDTH:, :]))
    for rs, y in zip(groups, ys):
        z = alpha * h_ref[0, rs, :] + mod_ref[0, 5:6, :] * y
        o_ref[0, rs, :] = _layernorm(z) * g1_ref[0] + b1_ref[0]
    for rs in groups:
        o_ref[0, rs, :] = _ffn_rows(o_ref[0, rs, :], mod_ref, 6, w_in_ref, w_out_ref, a_ref, rs, g2_ref, b2_ref,
                                    alpha)


def _pool_bands():
    t = jnp.arange(POOL_TOKENS)[:, None]
    j = jnp.arange(POOL_TOKENS + 2 * HALO)[None, :] - HALO
    inside = (j >= 0) & (j < POOL_TOKENS)
    bands = []
    for w in POOL_WINDOWS:
        lo, hi = w // 2, w - 1 - w // 2
        bands.append((j >= t - lo) & (j <= t + hi))
    bands = jnp.stack(bands)
    return jnp.stack([bands, bands & inside]).astype(BF16)


def _back(h, p, g_lat, g_ctx, mod, l, bands, pool_w, pool_scale, w_mix_out, ln_g, ln_b, w_in, w_out,
          *, n_batch, alpha, nseg):
    _, L, D = h.shape
    tm = BACK_TOKENS
    nt = L // tm
    hb = tm // HALO
    return pl.pallas_call(
        functools.partial(_back_kernel, n_batch=n_batch, alpha=alpha),
        grid=(nseg, nt),
        in_specs=[
            pl.BlockSpec((1, tm, D), lambda b, i: (b, i, 0)),
            pl.BlockSpec((1, tm, POOL_WIDTH), lambda b, i: (b, i, 0)),
            pl.BlockSpec((1, HALO, POOL_WIDTH), lambda b, i: (b, jnp.maximum(i * hb - 1, 0), 0)),
            pl.BlockSpec((1, HALO, POOL_WIDTH), lambda b, i: (b, jnp.minimum((i + 1) * hb, nt * hb - 1), 0)),
            pl.BlockSpec((1, tm, GLA_WIDTH),
                         lambda b, i: (jnp.minimum(b, n_batch - 1), jnp.where(b == n_batch, nt - 1, i), 0)),
            pl.BlockSpec((1, tm, GLA_WIDTH), lambda b, i: (0, jnp.where(b == n_batch, i, 0), 0)),
            pl.BlockSpec((1, N_MOD, D), lambda b, i: (b, 0, 0)),
            pl.BlockSpec((1,) + bands.shape[1:], lambda b, i: (jnp.where(b == n_batch, 1, 0), 0, 0, 0)),
            _layer_block(pool_w, l), _layer_block(pool_scale, l), _layer_block(w_mix_out, l),
            _layer_block(ln_g, 3 * l + 1), _layer_block(ln_b, 3 * l + 1),
            _layer_block(w_in, l), _layer_block(w_out, l),
            _layer_block(ln_g, 3 * l + 2), _layer_block(ln_b, 3 * l + 2),
        ],
        out_specs=pl.BlockSpec((1, tm, D), lambda b, i: (b, i, 0)),
        out_shape=jax.ShapeDtypeStruct((nseg, L, D), F32),
        scratch_shapes=[pltpu.VMEM((tm + 2 * HALO, POOL_WIDTH), BF16), pltpu.VMEM((tm, D_FF), BF16)],
        compiler_params=_params("arbitrary", "arbitrary"),
        name="back",
    )(h, p, p, p, g_lat, g_ctx, mod, bands, pool_w, pool_scale, w_mix_out, ln_g, ln_b, w_in, w_out, ln_g, ln_b)


def _pos_embed_2d(L):
    rows = L // GRID_W
    r = jnp.repeat(jnp.arange(rows, dtype=F32), GRID_W)
    col = jnp.tile(jnp.arange(GRID_W, dtype=F32), rows)
    quarter = D_MODEL // 4
    omega = 1.0 / (10000.0 ** (jnp.arange(quarter, dtype=F32) / quarter))

    def enc(p):
        a = p[:, None] * omega
        return jnp.concatenate([jnp.sin(a), jnp.cos(a)], axis=-1)

    return jnp.concatenate([enc(r), enc(col)], axis=-1)


def kernel(x, c, ctx, c_ctx, w_ada, b_ada, ln_g, ln_b, ffa_w_in, ffa_w_out, mix_w_in, pool_w, pool_scale,
           gate_up_f, gate_bias_f, gate_up_b, gate_bias_b, gla_norm_g, mix_w_out, ffb_w_in, ffb_w_out):
    B, L, D = x.shape
    LC = ctx.shape[1]
    depth = w_ada.shape[0]
    assert D == D_MODEL and LC == SEQ_TOKENS == POOL_TOKENS == FFN_ROWS == CUM_TOKENS and B * LC == L
    assert L % FFN_TOKENS == 0 and L % BACK_TOKENS == 0 and L % GLA_TOKENS == 0
    alpha = (2.0 * depth) ** 0.25
    nseg = B + 1

    rows = -(-nseg // 8) * 8
    cond = jnp.concatenate([c, c_ctx[None, :], jnp.zeros((rows - nseg, D), F32)], axis=0)
    mod = _ada(cond, w_ada, b_ada)[:, :nseg].reshape(depth, nseg, N_MOD, D)

    zeros_gd = jnp.zeros((depth, D, GD_PAD - 2 * GATE_RANK), F32)
    w_mix_in = jnp.concatenate([mix_w_in, zeros_gd], axis=-1).astype(BF16)
    gup = jnp.zeros((depth, GD_PAD, 2 * GLA_KW), F32)
    gup = gup.at[:, :GATE_RANK, :GLA_KW].set(gate_up_f)
    gup = gup.at[:, GATE_RANK:2 * GATE_RANK, GLA_KW:].set(gate_up_b).astype(BF16)
    gbias = jnp.concatenate([gate_bias_f, gate_bias_b], axis=-1).reshape(depth, 1, 2 * GLA_KW)
    ffa_in, ffa_out = ffa_w_in.astype(BF16), ffa_w_out.astype(BF16)
    ffb_in, ffb_out = ffb_w_in.astype(BF16), ffb_w_out.astype(BF16)
    w_mix_out = mix_w_out.astype(BF16)
    pw = pool_w.astype(BF16)
    ps = pool_scale.reshape(depth, 1, POOL_WIDTH)
    lng = ln_g.reshape(depth * 3, 1, D)
    lnb = ln_b.reshape(depth * 3, 1, D)
    ti = jnp.arange(CUM_TOKENS)
    same_chunk = (ti[:, None] // GLA_CHUNK) == (ti[None, :] // GLA_CHUNK)
    tri_f = (same_chunk & (ti[None, :] <= ti[:, None])).astype(BF16)
    tri_b = (same_chunk & (ti[None, :] >= ti[:, None])).astype(BF16)
    bands = _pool_bands()

    h = (x, _pos_embed_2d(L), ctx.reshape(1, L, D))
    for l in range(depth):
        last = l == depth - 1
        h, p, qk, tot = _front(h, mod[l], l, ffa_in, ffa_out, lng, lnb, w_mix_in, gup, gbias, tri_f, tri_b,
                               n_batch=B, alpha=alpha)
        g_ctx, s_ctx = _gla_ctx(p, qk, tot, gla_norm_g[l], n_batch=B)
        g_lat = _gla_latent(p, qk, tot, s_ctx, gla_norm_g[l], n_batch=B)
        h = _back(h, p, g_lat, g_ctx, mod[l], l, bands, pw, ps, w_mix_out, lng, lnb, ffb_in, ffb_out,
                  n_batch=B, alpha=alpha, nseg=B if last else nseg)
    return h
```

```python
import functools

import jax
import jax.numpy as jnp
from jax import lax
from jax.experimental import pallas as pl
from jax.experimental.pallas import tpu as pltpu

F32 = jnp.float32
BF16 = jnp.bfloat16

D_MODEL = 1024
D_FF = 2816
N_MOD = 9
POOL_WINDOWS = (2, 4, 8, 16)
POOL_GROUP = 128
POOL_WIDTH = POOL_GROUP * len(POOL_WINDOWS)
GLA_HEADS = 4
GLA_DK = 64
GLA_DV = 128
GLA_KW = GLA_HEADS * GLA_DK
GLA_WIDTH = GLA_HEADS * GLA_DV
GATE_RANK = 16
GATE_TAU = 16.0
GLA_CHUNK = 64
GRID_W = 64
LN_EPS = 1e-6
MAIN_COLS = POOL_WIDTH + 2 * GLA_KW + 2 * GLA_WIDTH
P_COLS = POOL_WIDTH + 2 * GLA_WIDTH
QK_COLS = 3 * GLA_KW
GD_PAD = 256

VMEM_LIMIT_BYTES = 56 * 1024 * 1024
FFN_TOKENS = 512
BACK_TOKENS = 1024
FFN_CHUNK = 256
FFN_ROWS = 256
SEQ_TOKENS = 256
GLA_TOKENS = 1024
SUB_TOKENS = 128
GLA_GROUP = 2
CUM_TOKENS = 256
POOL_TOKENS = 256
HALO = 16


def _dot(a, b):
    return jnp.dot(a, b, preferred_element_type=F32)


def _dot_nt(a, b):
    return lax.dot_general(a, b, (((1,), (1,)), ((), ())), preferred_element_type=F32)


def _dot_tn(a, b):
    return lax.dot_general(a, b, (((0,), (0,)), ((), ())), preferred_element_type=F32)


def _silu(x):
    return x / (1.0 + jnp.exp(-x))


def _layernorm(z):
    mu = jnp.mean(z, axis=-1, keepdims=True)
    zc = z - mu
    var = jnp.mean(zc * zc, axis=-1, keepdims=True)
    return zc * lax.rsqrt(var + LN_EPS)


def _params(*sem):
    return pltpu.CompilerParams(dimension_semantics=sem, vmem_limit_bytes=VMEM_LIMIT_BYTES)


def _resident(shape):
    nd = len(shape)
    return pl.BlockSpec(shape, lambda *_: (0,) * nd, pipeline_mode=pl.Buffered(1))


def _ada_kernel(c_ref, w_ref, b_ref, o_ref):
    s = _silu(c_ref[...])
    o_ref[0] = _dot(s.astype(BF16), w_ref[0].astype(BF16)) + b_ref[0]


def _ada(cond, w_ada, b_ada):
    depth, d, n = w_ada.shape
    rows = cond.shape[0]
    tn = 1024
    return pl.pallas_call(
        _ada_kernel,
        grid=(depth, n // tn),
        in_specs=[
            pl.BlockSpec((rows, d), lambda l, j: (0, 0)),
            pl.BlockSpec((1, d, tn), lambda l, j: (l, 0, j)),
            pl.BlockSpec((1, 1, tn), lambda l, j: (l, 0, j)),
        ],
        out_specs=pl.BlockSpec((1, rows, tn), lambda l, j: (l, 0, j)),
        out_shape=jax.ShapeDtypeStruct((depth, rows, n), F32),
        compiler_params=_params("arbitrary", "arbitrary"),
        name="ada",
    )(cond, w_ada, b_ada.reshape(depth, 1, n))


def _ffn_rows(xr, mod_ref, k0, w_in_ref, w_out_ref, a_ref, rs, g_ref, b_ref, alpha):
    shift = mod_ref[0, k0:k0 + 1, :]
    scale = mod_ref[0, k0 + 1:k0 + 2, :]
    gate = mod_ref[0, k0 + 2:k0 + 3, :]
    xm = (xr * (1.0 + scale) + shift).astype(BF16)
    for j in range(D_FF // FFN_CHUNK):
        lo = j * FFN_CHUNK
        g = _dot(xm, w_in_ref[0, :, lo:lo + FFN_CHUNK])
        u = _dot(xm, w_in_ref[0, :, D_FF + lo:D_FF + lo + FFN_CHUNK])
        a_ref[rs, lo:lo + FFN_CHUNK] = (_silu(g) * u).astype(BF16)
    y = _dot(a_ref[rs, :], w_out_ref[0])
    z = alpha * xr + (0.5 * gate) * y
    return _layernorm(z) * g_ref[0] + b_ref[0]


def _inproj_groups(h_ref, groups, mod_ref, w_ref, gup_ref, gb_ref, trif_ref, trib_ref, p_ref, qk_ref, tot_ref):
    shift = mod_ref[0, 3:4, :]
    scale = mod_ref[0, 4:5, :]
    xms = [(h_ref[0, rs, :] * (1.0 + scale) + shift).astype(BF16) for rs in groups]
    gds = [_dot(xm, w_ref[0, :, MAIN_COLS:MAIN_COLS + GD_PAD]) for xm in xms]
    zs = [_dot(gd.astype(BF16), gup_ref[0]) + gb_ref[0] for gd in gds]
    cums = []
    for z in zs:
        la = (jnp.minimum(z, 0.0) - jnp.log1p(jnp.exp(-jnp.abs(z)))) * (1.0 / GATE_TAU)
        la_hi = la.astype(BF16)
        la_lo = (la - la_hi.astype(F32)).astype(BF16)
        cums.append([_dot(tri_ref[...], la_hi[:, d * GLA_KW:(d + 1) * GLA_KW])
                     + _dot(tri_ref[...], la_lo[:, d * GLA_KW:(d + 1) * GLA_KW])
                     for d, tri_ref in enumerate((trif_ref, trib_ref))])
    qks = [_dot(xm, w_ref[0, :, POOL_WIDTH:POOL_WIDTH + 2 * GLA_KW]) for xm in xms]
    for rs, xm in zip(groups, xms):
        p_ref[0, rs, 0:POOL_WIDTH] = _dot(xm, w_ref[0, :, 0:POOL_WIDTH]).astype(BF16)
        for j in (1, 2):
            p_ref[0, rs, j * 512:(j + 1) * 512] = _dot(xm, w_ref[0, :, 512 + j * 512:1024 + j * 512]).astype(BF16)
    for r, (rs, qk) in enumerate(zip(groups, qks)):
        q = qk[:, :GLA_KW] * (GLA_DK ** -0.5)
        k = qk[:, GLA_KW:]
        for d, bcs in enumerate(cums[r]):
            cs = slice(d * GLA_KW, (d + 1) * GLA_KW)
            tots = []
            for c in range(CUM_TOKENS // GLA_CHUNK):
                e = c * GLA_CHUNK + (GLA_CHUNK - 1 if d == 0 else 0)
                t = bcs[e:e + 1, :]
                row = r * (CUM_TOKENS // GLA_CHUNK) + c
                tot_ref[0, row:row + 1, cs] = t
                tots.append(jnp.broadcast_to(t, (GLA_CHUNK, GLA_KW)))
            tot = jnp.concatenate(tots, axis=0)
            base = d * QK_COLS
            qk_ref[0, rs, base:base + GLA_KW] = (q * jnp.exp(bcs)).astype(BF16)
            qk_ref[0, rs, base + GLA_KW:base + 2 * GLA_KW] = (k * jnp.exp(-bcs)).astype(BF16)
            qk_ref[0, rs, base + 2 * GLA_KW:base + 3 * GLA_KW] = (k * jnp.exp(tot - bcs)).astype(BF16)


def _front_kernel(*refs, n_batch, alpha, first_layer):
    if first_layer:
        x_ref, pos_ref, ctx_ref = refs[:3]
        refs = refs[3:]
    else:
        h_ref = refs[0]
        refs = refs[1:]
    (mod_ref, w_in_ref, w_out_ref, g_ref, b_ref, wmix_ref, gup_ref, gb_ref, trif_ref, trib_ref,
     o_ref, p_ref, qk_ref, tot_ref, a_ref) = refs
    is_ctx = pl.program_id(0) == n_batch
    groups = [slice(r * FFN_ROWS, (r + 1) * FFN_ROWS) for r in range(o_ref.shape[1] // FFN_ROWS)]
    for rs in groups:
        if first_layer:
            xr = _layernorm(jnp.where(is_ctx, ctx_ref[0, rs, :], x_ref[0, rs, :] + pos_ref[rs, :]))
        else:
            xr = h_ref[0, rs, :]
        o_ref[0, rs, :] = _ffn_rows(xr, mod_ref, 0, w_in_ref, w_out_ref, a_ref, rs, g_ref, b_ref, alpha)
    _inproj_groups(o_ref, groups, mod_ref, wmix_ref, gup_ref, gb_ref, trif_ref, trib_ref, p_ref, qk_ref, tot_ref)


def _layer_block(arr, l):
    nd = arr.ndim
    return pl.BlockSpec((1,) + arr.shape[1:], lambda *_: (l,) + (0,) * (nd - 1), pipeline_mode=pl.Buffered(1))


def _front(src, mod, l, w_in, w_out, ln_g, ln_b, w_mix, gup, gbias, tri_f, tri_b, *, n_batch, alpha):
    first_layer = isinstance(src, tuple)
    B = n_batch
    nseg = B + 1
    tm = FFN_TOKENS
    if first_layer:
        x, pos, ctx_flat = src
        _, L, D = x.shape
        nt = L // tm
        src_specs = [
            pl.BlockSpec((1, tm, D), lambda b, i: (jnp.minimum(b, B - 1), jnp.where(b == B, nt - 1, i), 0)),
            pl.BlockSpec((tm, D), lambda b, i: (jnp.where(b == B, nt - 1, i), 0)),
            pl.BlockSpec((1, tm, D), lambda b, i: (0, jnp.where(b == B, i, 0), 0)),
        ]
    else:
        src = (src,)
        _, L, D = src[0].shape
        nt = L // tm
        src_specs = [pl.BlockSpec((1, tm, D), lambda b, i: (b, i, 0))]
    return pl.pallas_call(
        functools.partial(_front_kernel, n_batch=B, alpha=alpha, first_layer=first_layer),
        grid=(nseg, nt),
        in_specs=src_specs + [
            pl.BlockSpec((1, N_MOD, D), lambda b, i: (b, 0, 0)),
            _layer_block(w_in, l), _layer_block(w_out, l), _layer_block(ln_g, 3 * l), _layer_block(ln_b, 3 * l),
            _layer_block(w_mix, l), _layer_block(gup, l), _layer_block(gbias, l),
            _resident(tri_f.shape), _resident(tri_b.shape),
        ],
        out_specs=[
            pl.BlockSpec((1, tm, D), lambda b, i: (b, i, 0)),
            pl.BlockSpec((1, tm, P_COLS), lambda b, i: (b, i, 0)),
            pl.BlockSpec((1, tm, 2 * QK_COLS), lambda b, i: (b, i, 0)),
            pl.BlockSpec((1, tm // GLA_CHUNK, 2 * GLA_KW), lambda b, i: (b, i, 0)),
        ],
        out_shape=[
            jax.ShapeDtypeStruct((nseg, L, D), F32),
            jax.ShapeDtypeStruct((nseg, L, P_COLS), BF16),
            jax.ShapeDtypeStruct((nseg, L, 2 * QK_COLS), BF16),
            jax.ShapeDtypeStruct((nseg, L // GLA_CHUNK, 2 * GLA_KW), F32),
        ],
        scratch_shapes=[pltpu.VMEM((tm, D_FF), BF16)],
        compiler_params=_params("arbitrary", "arbitrary"),
        name="front",
    )(*src, mod, w_in, w_out, ln_g, ln_b, w_mix, gup, gbias, tri_f, tri_b)


def _gla_block(qk_ref, v_ref, tot_row, st_ref, *, fwd, tokens):
    pair = 2 * GLA_DK
    d = 0 if fwd else 1
    lane = lax.broadcasted_iota(jnp.int32, (pair, pair), 1)
    first_head = lane < GLA_DK
    head_mask = [jnp.where(first_head, 1.0, 0.0).astype(BF16), jnp.where(first_head, 0.0, 1.0).astype(BF16)]
    srow = lax.broadcasted_iota(jnp.int32, (2 * SUB_TOKENS, SUB_TOKENS), 0) % SUB_TOKENS
    scol = lax.broadcasted_iota(jnp.int32, (2 * SUB_TOKENS, SUB_TOKENS), 1)
    keep = ((srow // GLA_CHUNK) == (scol // GLA_CHUNK)) & ((scol <= srow) if fwd else (scol >= srow))

    nchunk = tokens // GLA_CHUNK
    npair = GLA_HEADS // 2
    chunk_rows = [slice(c * GLA_CHUNK, (c + 1) * GLA_CHUNK) for c in range(nchunk)]
    sub_rows = [slice(sb * SUB_TOKENS, (sb + 1) * SUB_TOKENS) for sb in range(tokens // SUB_TOKENS)]

    upd = {(c, p): _dot_tn(v_ref[0, rs, 2 * p * GLA_DV:(2 * p + 2) * GLA_DV],
                           qk_ref[0, rs, 2 * GLA_KW + p * pair:2 * GLA_KW + (p + 1) * pair])
           for c, rs in enumerate(chunk_rows) for p in range(npair)}
    st = [st_ref[p * pair:(p + 1) * pair, :] for p in range(npair)]
    st_start = {}
    for c in (range(nchunk) if fwd else range(nchunk - 1, -1, -1)):
        for p in range(npair):
            st_start[c, p] = st[p].astype(BF16)
            decay = jnp.exp(tot_row(c, slice(d * GLA_KW + p * pair, d * GLA_KW + (p + 1) * pair)))
            u = upd[c, p]
            st[p] = st[p] * decay + jnp.where(first_head, u[:GLA_DV], u[GLA_DV:])
    for p in range(npair):
        st_ref[p * pair:(p + 1) * pair, :] = st[p]

    zeros_half = jnp.zeros((GLA_CHUNK, pair), BF16)
    group_size = len(sub_rows) if fwd else GLA_GROUP
    for g0 in range(0, len(sub_rows), group_size):
        group = list(range(g0, min(g0 + group_size, len(sub_rows))))
        q_h = {}
        scores = {}
        for sb in group:
            rs = sub_rows[sb]
            for p in range(npair):
                qp = qk_ref[0, rs, p * pair:(p + 1) * pair]
                q_h[sb, p] = [qp * head_mask[0], qp * head_mask[1]]
                k_in = qk_ref[0, rs, GLA_KW + p * pair:GLA_KW + (p + 1) * pair]
                scores[sb, p] = _dot_nt(jnp.concatenate(q_h[sb, p], axis=0), k_in)
        inter = {}
        for sb in group:
            for p in range(npair):
                c0 = sb * (SUB_TOKENS // GLA_CHUNK)
                st_cat = jnp.concatenate([st_start[c0, p], st_start[c0 + 1, p]], axis=1)
                q0, q1 = q_h[sb, p]
                q_inter = jnp.concatenate(
                    [jnp.concatenate([q0[:GLA_CHUNK], zeros_half, q1[:GLA_CHUNK], zeros_half], axis=0),
                     jnp.concatenate([zeros_half, q0[GLA_CHUNK:], zeros_half, q1[GLA_CHUNK:]], axis=0)], axis=1)
                inter[sb, p] = _dot_nt(q_inter, st_cat)
        for sb in group:
            rs = sub_rows[sb]
            o_heads = []
            for p in range(npair):
                sc = jnp.where(keep, scores[sb, p], 0.0).astype(BF16)
                for hh in range(2):
                    h = 2 * p + hh
                    hs = slice(hh * SUB_TOKENS, (hh + 1) * SUB_TOKENS)
                    o_heads.append(_dot(sc[hs], v_ref[0, rs, h * GLA_DV:(h + 1) * GLA_DV]) + inter[sb, p][hs])
            yield sb, jnp.concatenate(o_heads, axis=1)


def _gla_kernel(*refs, tokens, nblk, is_ctx):
    if is_ctx:
        qkf_ref, qkb_ref, v_ref, r_ref, tot_ref, ng_ref, o_ref, sfin_ref, of_ref, st_ref = refs
    else:
        qkf_ref, qkb_ref, v_ref, r_ref, tot_ref, s0_ref, ng_ref, o_ref, of_ref, st_ref = refs
    ph = pl.program_id(1)
    j = pl.program_id(2)
    jb = jnp.where(ph == 0, j, nblk - 1 - j)
    base = pl.multiple_of(jb * tokens, tokens)
    nchunk = tokens // GLA_CHUNK

    @pl.when(j == 0)
    def _():
        st_ref[...] = jnp.zeros_like(st_ref) if is_ctx else s0_ref[0, 0]

    if is_ctx:
        odd = (pl.program_id(0) % 2) == 1

        def tot_row(c, ls):
            return jnp.where(odd, tot_ref[0, nchunk + c:nchunk + c + 1, ls], tot_ref[0, c:c + 1, ls])
    else:
        def tot_row(c, ls):
            return tot_ref[0, c:c + 1, ls]

    @pl.when(ph == 0)
    def _():
        for sb, o in _gla_block(qkf_ref, v_ref, tot_row, st_ref, fwd=True, tokens=tokens):
            of_ref[pl.ds(base + sb * SUB_TOKENS, SUB_TOKENS), :] = o

    @pl.when(ph == 1)
    def _():
        def finish(sb, o):
            rs = slice(sb * SUB_TOKENS, (sb + 1) * SUB_TOKENS)
            o = o + of_ref[pl.ds(base + sb * SUB_TOKENS, SUB_TOKENS), :]
            normed = []
            for h in range(GLA_HEADS):
                oh = o[:, h * GLA_DV:(h + 1) * GLA_DV]
                normed.append(oh * lax.rsqrt(jnp.mean(oh * oh, axis=-1, keepdims=True) + LN_EPS))
            on = jnp.concatenate(normed, axis=1) * ng_ref[...]
            o_ref[0, rs, :] = (on * _silu(r_ref[0, rs, :].astype(F32))).astype(BF16)

        pending = None
        for item in _gla_block(qkb_ref, v_ref, tot_row, st_ref, fwd=False, tokens=tokens):
            if pending is not None:
                finish(*pending)
            pending = item
        finish(*pending)

    if is_ctx:
        @pl.when(j == nblk - 1)
        def _():
            sfin_ref[0, 0] = st_ref[...]


def _gla_ctx(p, qk, tot, norm_g, *, n_batch):
    _, L, _ = p.shape
    tb = SEQ_TOKENS
    st_shape = ((GLA_HEADS // 2) * GLA_DV, 2 * GLA_DK)
    return pl.pallas_call(
        functools.partial(_gla_kernel, tokens=tb, nblk=1, is_ctx=True),
        grid=(n_batch, 2, 1),
        in_specs=[
            pl.BlockSpec((1, tb, QK_COLS), lambda b, ph, j: (n_batch, b, 0)),
            pl.BlockSpec((1, tb, QK_COLS), lambda b, ph, j: (n_batch, b, 1)),
            pl.BlockSpec((1, tb, GLA_WIDTH), lambda b, ph, j: (n_batch, b, 1)),
            pl.BlockSpec((1, tb, GLA_WIDTH), lambda b, ph, j: (n_batch, b, 2)),
            pl.BlockSpec((1, 2 * tb // GLA_CHUNK, 2 * GLA_KW), lambda b, ph, j: (n_batch, b // 2, 0)),
            _resident((1, GLA_WIDTH)),
        ],
        out_specs=[
            pl.BlockSpec((1, tb, GLA_WIDTH), lambda b, ph, j: (0, b, 0)),
            pl.BlockSpec((1, 1) + st_shape, lambda b, ph, j: (b, ph, 0, 0)),
        ],
        out_shape=[
            jax.ShapeDtypeStruct((1, L, GLA_WIDTH), BF16),
            jax.ShapeDtypeStruct((n_batch, 2) + st_shape, F32),
        ],
        scratch_shapes=[
            pltpu.VMEM((tb, GLA_WIDTH), F32),
            pltpu.VMEM(st_shape, F32),
        ],
        compiler_params=_params("arbitrary", "arbitrary", "arbitrary"),
        name="gla_ctx",
    )(qk, qk, p, p, tot, norm_g.reshape(1, GLA_WIDTH))


def _gla_latent(p, qk, tot, s0, norm_g, *, n_batch):
    _, L, _ = p.shape
    tb = GLA_TOKENS
    nblk = L // tb
    st_shape = ((GLA_HEADS // 2) * GLA_DV, 2 * GLA_DK)

    def visited(ph, j):
        return jnp.where(ph == 0, j, nblk - 1 - j)

    def fwd_only(cidx):
        return lambda b, ph, j: (b, jnp.where(ph == 0, j, nblk - 1), cidx)

    def bwd_only(cidx):
        return lambda b, ph, j: (b, jnp.where(ph == 0, nblk - 1, nblk - 1 - j), cidx)

    return pl.pallas_call(
        functools.partial(_gla_kernel, tokens=tb, nblk=nblk, is_ctx=False),
        grid=(n_batch, 2, nblk),
        in_specs=[
            pl.BlockSpec((1, tb, QK_COLS), fwd_only(0)),
            pl.BlockSpec((1, tb, QK_COLS), bwd_only(1)),
            pl.BlockSpec((1, tb, GLA_WIDTH), lambda b, ph, j: (b, visited(ph, j), 1)),
            pl.BlockSpec((1, tb, GLA_WIDTH), bwd_only(2)),
            pl.BlockSpec((1, tb // GLA_CHUNK, 2 * GLA_KW), lambda b, ph, j: (b, visited(ph, j), 0)),
            pl.BlockSpec((1, 1) + st_shape, lambda b, ph, j: (b, ph, 0, 0)),
            _resident((1, GLA_WIDTH)),
        ],
        out_specs=pl.BlockSpec((1, tb, GLA_WIDTH), bwd_only(0)),
        out_shape=jax.ShapeDtypeStruct((n_batch, L, GLA_WIDTH), BF16),
        scratch_shapes=[
            pltpu.VMEM((L, GLA_WIDTH), F32),
            pltpu.VMEM(st_shape, F32),
        ],
        compiler_params=_params("arbitrary", "arbitrary", "arbitrary"),
        name="gla_latent",
    )(qk, qk, p, p, tot, s0, norm_g.reshape(1, GLA_WIDTH))


def _back_kernel(h_ref, u_ref, up_ref, un_ref, glat_ref, gctx_ref, mod_ref, band_ref, pw_ref, ps_ref, wmix_ref,
                 g1_ref, b1_ref, w_in_ref, w_out_ref, g2_ref, b2_ref, o_ref, ue_ref, a_ref, *, n_batch, alpha):
    tm = h_ref.shape[1]
    i = pl.program_id(1)
    is_ctx = pl.program_id(0) == n_batch
    first = is_ctx | (i == 0)
    last = is_ctx | (i == pl.num_programs(1) - 1)
    no_halo = jnp.zeros((HALO, POOL_WIDTH), BF16)
    ue_ref[0:HALO, :] = jnp.where(first, no_halo, up_ref[0])
    ue_ref[HALO:HALO + tm, :] = u_ref[0]
    ue_ref[HALO + tm:, :] = jnp.where(last, no_halo, un_ref[0])

    pos = lax.broadcasted_iota(jnp.int32, (POOL_TOKENS, 1), 0)
    nsub = tm // POOL_TOKENS

    groups = [slice(s * POOL_TOKENS, (s + 1) * POOL_TOKENS) for s in range(nsub)]
    wsums = [[_dot(band_ref[0, g], ue_ref[rs.start:rs.stop + 2 * HALO, g * POOL_GROUP:(g + 1) * POOL_GROUP])
              for g in range(len(POOL_WINDOWS))] for rs in groups]
    pool_ys = []
    for s, rs in enumerate(groups):
        starts = first if s == 0 else is_ctx
        ends = last if s == nsub - 1 else is_ctx
        room_lo = jnp.where(starts, pos, POOL_TOKENS)
        room_hi = jnp.where(ends, POOL_TOKENS - 1 - pos, POOL_TOKENS)
        yg = []
        for g, w in enumerate(POOL_WINDOWS):
            cs = slice(g * POOL_GROUP, (g + 1) * POOL_GROUP)
            lo, hi = w // 2, w - 1 - w // 2
            cnt = (jnp.minimum(room_lo, lo) + jnp.minimum(room_hi, hi) + 1).astype(F32)
            pooled = wsums[s][g] / cnt - ue_ref[HALO + rs.start:HALO + rs.stop, cs].astype(F32)
            yg.append(_dot(pooled.astype(BF16), pw_ref[0, g]))
        pool_ys.append((jnp.concatenate(yg, axis=1) * ps_ref[0]).astype(BF16))
    ys = []
    for rs, pool_y in zip(groups, pool_ys):
        gla = jnp.where(is_ctx, gctx_ref[0, rs, :], glat_ref[0, rs, :])
        ys.append(_dot(pool_y, wmix_ref[0, :POOL_WIDTH, :]) + _dot(gla, wmix_ref[0, POOL_WIDTH:, :]))
    for rs, y in zip(groups, ys):
        z = alpha * h_ref[0, rs, :] + mod_ref[0, 5:6, :] * y
        o_ref[0, rs, :] = _layernorm(z) * g1_ref[0] + b1_ref[0]
    for rs in groups:
        o_ref[0, rs, :] = _ffn_rows(o_ref[0, rs, :], mod_ref, 6, w_in_ref, w_out_ref, a_ref, rs, g2_ref, b2_ref,
                                    alpha)


def _pool_bands():
    t = jnp.arange(POOL_TOKENS)[:, None]
    j = jnp.arange(POOL_TOKENS + 2 * HALO)[None, :] - HALO
    inside = (j >= 0) & (j < POOL_TOKENS)
    bands = []
    for w in POOL_WINDOWS:
        lo, hi = w // 2, w - 1 - w // 2
        bands.append((j >= t - lo) & (j <= t + hi))
    bands = jnp.stack(bands)
    return jnp.stack([bands, bands & inside]).astype(BF16)


def _back(h, p, g_lat, g_ctx, mod, l, bands, pool_w, pool_scale, w_mix_out, ln_g, ln_b, w_in, w_out,
          *, n_batch, alpha, nseg):
    _, L, D = h.shape
    tm = BACK_TOKENS
    nt = L // tm
    hb = tm // HALO
    return pl.pallas_call(
        functools.partial(_back_kernel, n_batch=n_batch, alpha=alpha),
        grid=(nseg, nt),
        in_specs=[
            pl.BlockSpec((1, tm, D), lambda b, i: (b, i, 0)),
            pl.BlockSpec((1, tm, POOL_WIDTH), lambda b, i: (b, i, 0)),
            pl.BlockSpec((1, HALO, POOL_WIDTH), lambda b, i: (b, jnp.maximum(i * hb - 1, 0), 0)),
            pl.BlockSpec((1, HALO, POOL_WIDTH), lambda b, i: (b, jnp.minimum((i + 1) * hb, nt * hb - 1), 0)),
            pl.BlockSpec((1, tm, GLA_WIDTH),
                         lambda b, i: (jnp.minimum(b, n_batch - 1), jnp.where(b == n_batch, nt - 1, i), 0)),
            pl.BlockSpec((1, tm, GLA_WIDTH), lambda b, i: (0, jnp.where(b == n_batch, i, 0), 0)),
            pl.BlockSpec((1, N_MOD, D), lambda b, i: (b, 0, 0)),
            pl.BlockSpec((1,) + bands.shape[1:], lambda b, i: (jnp.where(b == n_batch, 1, 0), 0, 0, 0)),
            _layer_block(pool_w, l), _layer_block(pool_scale, l), _layer_block(w_mix_out, l),
            _layer_block(ln_g, 3 * l + 1), _layer_block(ln_b, 3 * l + 1),
            _layer_block(w_in, l), _layer_block(w_out, l),
            _layer_block(ln_g, 3 * l + 2), _layer_block(ln_b, 3 * l + 2),
        ],
        out_specs=pl.BlockSpec((1, tm, D), lambda b, i: (b, i, 0)),
        out_shape=jax.ShapeDtypeStruct((nseg, L, D), F32),
        scratch_shapes=[pltpu.VMEM((tm + 2 * HALO, POOL_WIDTH), BF16), pltpu.VMEM((tm, D_FF), BF16)],
        compiler_params=_params("arbitrary", "arbitrary"),
        name="back",
    )(h, p, p, p, g_lat, g_ctx, mod, bands, pool_w, pool_scale, w_mix_out, ln_g, ln_b, w_in, w_out, ln_g, ln_b)


def _pos_embed_2d(L):
    rows = L // GRID_W
    r = jnp.repeat(jnp.arange(rows, dtype=F32), GRID_W)
    col = jnp.tile(jnp.arange(GRID_W, dtype=F32), rows)
    quarter = D_MODEL // 4
    omega = 1.0 / (10000.0 ** (jnp.arange(quarter, dtype=F32) / quarter))

    def enc(p):
        a = p[:, None] * omega
        return jnp.concatenate([jnp.sin(a), jnp.cos(a)], axis=-1)

    return jnp.concatenate([enc(r), enc(col)], axis=-1)


def kernel(x, c, ctx, c_ctx, w_ada, b_ada, ln_g, ln_b, ffa_w_in, ffa_w_out, mix_w_in, pool_w, pool_scale,
           gate_up_f, gate_bias_f, gate_up_b, gate_bias_b, gla_norm_g, mix_w_out, ffb_w_in, ffb_w_out):
    B, L, D = x.shape
    LC = ctx.shape[1]
    depth = w_ada.shape[0]
    assert D == D_MODEL and LC == SEQ_TOKENS == POOL_TOKENS == FFN_ROWS == CUM_TOKENS and B * LC == L
    assert L % FFN_TOKENS == 0 and L % BACK_TOKENS == 0 and L % GLA_TOKENS == 0
    alpha = (2.0 * depth) ** 0.25
    nseg = B + 1

    rows = -(-nseg // 8) * 8
    cond = jnp.concatenate([c, c_ctx[None, :], jnp.zeros((rows - nseg, D), F32)], axis=0)
    mod = _ada(cond, w_ada, b_ada)[:, :nseg].reshape(depth, nseg, N_MOD, D)

    zeros_gd = jnp.zeros((depth, D, GD_PAD - 2 * GATE_RANK), F32)
    w_mix_in = jnp.concatenate([mix_w_in, zeros_gd], axis=-1).astype(BF16)
    gup = jnp.zeros((depth, GD_PAD, 2 * GLA_KW), F32)
    gup = gup.at[:, :GATE_RANK, :GLA_KW].set(gate_up_f)
    gup = gup.at[:, GATE_RANK:2 * GATE_RANK, GLA_KW:].set(gate_up_b).astype(BF16)
    gbias = jnp.concatenate([gate_bias_f, gate_bias_b], axis=-1).reshape(depth, 1, 2 * GLA_KW)
    ffa_in, ffa_out = ffa_w_in.astype(BF16), ffa_w_out.astype(BF16)
    ffb_in, ffb_out = ffb_w_in.astype(BF16), ffb_w_out.astype(BF16)
    w_mix_out = mix_w_out.astype(BF16)
    pw = pool_w.astype(BF16)
    ps = pool_scale.reshape(depth, 1, POOL_WIDTH)
    lng = ln_g.reshape(depth * 3, 1, D)
    lnb = ln_b.reshape(depth * 3, 1, D)
    ti = jnp.arange(CUM_TOKENS)
    same_chunk = (ti[:, None] // GLA_CHUNK) == (ti[None, :] // GLA_CHUNK)
    tri_f = (same_chunk & (ti[None, :] <= ti[:, None])).astype(BF16)
    tri_b = (same_chunk & (ti[None, :] >= ti[:, None])).astype(BF16)
    bands = _pool_bands()

    h = (x, _pos_embed_2d(L), ctx.reshape(1, L, D))
    for l in range(depth):
        last = l == depth - 1
        h, p, qk, tot = _front(h, mod[l], l, ffa_in, ffa_out, lng, lnb, w_mix_in, gup, gbias, tri_f, tri_b,
                               n_batch=B, alpha=alpha)
        g_ctx, s_ctx = _gla_ctx(p, qk, tot, gla_norm_g[l], n_batch=B)
        g_lat = _gla_latent(p, qk, tot, s_ctx, gla_norm_g[l], n_batch=B)
        h = _back(h, p, g_lat, g_ctx, mod[l], l, bands, pw, ps, w_mix_out, lng, lnb, ffb_in, ffb_out,
                  n_batch=B, alpha=alpha, nseg=B if last else nseg)
    return h
```

```python
import functools

import jax
import jax.numpy as jnp
from jax import lax
from jax.experimental import pallas as pl
from jax.experimental.pallas import tpu as pltpu

F32 = jnp.float32
BF16 = jnp.bfloat16

D_MODEL = 1024
D_FF = 2816
N_MOD = 9
POOL_WINDOWS = (2, 4, 8, 16)
POOL_GROUP = 128
POOL_WIDTH = POOL_GROUP * len(POOL_WINDOWS)
GLA_HEADS = 4
GLA_DK = 64
GLA_DV = 128
GLA_KW = GLA_HEADS * GLA_DK
GLA_WIDTH = GLA_HEADS * GLA_DV
GATE_RANK = 16
GATE_TAU = 16.0
GLA_CHUNK = 64
GRID_W = 64
LN_EPS = 1e-6
MAIN_COLS = POOL_WIDTH + 2 * GLA_KW + 2 * GLA_WIDTH
P_COLS = POOL_WIDTH + 2 * GLA_WIDTH
QK_COLS = 3 * GLA_KW
GD_PAD = 256

VMEM_LIMIT_BYTES = 56 * 1024 * 1024
FFN_TOKENS = 512
BACK_TOKENS = 1024
FFN_CHUNK = 256
FFN_ROWS = 256
SEQ_TOKENS = 256
GLA_TOKENS = 1024
SUB_TOKENS = 128
GLA_GROUP = 2
CUM_TOKENS = 256
POOL_TOKENS = 256
HALO = 16


def _dot(a, b):
    return jnp.dot(a, b, preferred_element_type=F32)


def _dot_nt(a, b):
    return lax.dot_general(a, b, (((1,), (1,)), ((), ())), preferred_element_type=F32)


def _dot_tn(a, b):
    return lax.dot_general(a, b, (((0,), (0,)), ((), ())), preferred_element_type=F32)


def _silu(x):
    return x / (1.0 + jnp.exp(-x))


def _layernorm(z):
    mu = jnp.mean(z, axis=-1, keepdims=True)
    zc = z - mu
    var = jnp.mean(zc * zc, axis=-1, keepdims=True)
    return zc * lax.rsqrt(var + LN_EPS)


def _params(*sem):
    return pltpu.CompilerParams(dimension_semantics=sem, vmem_limit_bytes=VMEM_LIMIT_BYTES)


def _resident(shape):
    nd = len(shape)
    return pl.BlockSpec(shape, lambda *_: (0,) * nd, pipeline_mode=pl.Buffered(1))


def _ada_kernel(c_ref, w_ref, b_ref, o_ref):
    s = _silu(c_ref[...])
    o_ref[0] = _dot(s.astype(BF16), w_ref[0].astype(BF16)) + b_ref[0]


def _ada(cond, w_ada, b_ada):
    depth, d, n = w_ada.shape
    rows = cond.shape[0]
    tn = 1024
    return pl.pallas_call(
        _ada_kernel,
        grid=(depth, n // tn),
        in_specs=[
            pl.BlockSpec((rows, d), lambda l, j: (0, 0)),
            pl.BlockSpec((1, d, tn), lambda l, j: (l, 0, j)),
            pl.BlockSpec((1, 1, tn), lambda l, j: (l, 0, j)),
        ],
        out_specs=pl.BlockSpec((1, rows, tn), lambda l, j: (l, 0, j)),
        out_shape=jax.ShapeDtypeStruct((depth, rows, n), F32),
        compiler_params=_params("arbitrary", "arbitrary"),
        name="ada",
    )(cond, w_ada, b_ada.reshape(depth, 1, n))


def _ffn_rows(xr, mod_ref, k0, w_in_ref, w_out_ref, a_ref, rs, g_ref, b_ref, alpha):
    shift = mod_ref[0, k0:k0 + 1, :]
    scale = mod_ref[0, k0 + 1:k0 + 2, :]
    gate = mod_ref[0, k0 + 2:k0 + 3, :]
    xm = (xr * (1.0 + scale) + shift).astype(BF16)
    for j in range(D_FF // FFN_CHUNK):
        lo = j * FFN_CHUNK
        g = _dot(xm, w_in_ref[0, :, lo:lo + FFN_CHUNK])
        u = _dot(xm, w_in_ref[0, :, D_FF + lo:D_FF + lo + FFN_CHUNK])
        a_ref[rs, lo:lo + FFN_CHUNK] = (_silu(g) * u).astype(BF16)
    y = _dot(a_ref[rs, :], w_out_ref[0])
    z = alpha * xr + (0.5 * gate) * y
    return _layernorm(z) * g_ref[0] + b_ref[0]


def _inproj_groups(h_ref, groups, mod_ref, w_ref, gup_ref, gb_ref, trif_ref, trib_ref, p_ref, qk_ref, tot_ref):
    shift = mod_ref[0, 3:4, :]
    scale = mod_ref[0, 4:5, :]
    xms = [(h_ref[0, rs, :] * (1.0 + scale) + shift).astype(BF16) for rs in groups]
    gds = [_dot(xm, w_ref[0, :, MAIN_COLS:MAIN_COLS + GD_PAD]) for xm in xms]
    zs = [_dot(gd.astype(BF16), gup_ref[0]) + gb_ref[0] for gd in gds]

    def wide(r):
        rs, xm = groups[r], xms[r]
        qk = _dot(xm, w_ref[0, :, POOL_WIDTH:POOL_WIDTH + 2 * GLA_KW])
        p_ref[0, rs, 0:POOL_WIDTH] = _dot(xm, w_ref[0, :, 0:POOL_WIDTH]).astype(BF16)
        for j in (1, 2):
            p_ref[0, rs, j * 512:(j + 1) * 512] = _dot(xm, w_ref[0, :, 512 + j * 512:1024 + j * 512]).astype(BF16)
        return qk

    qks = [wide(0)]
    cums = []
    for z in zs:
        la = (jnp.minimum(z, 0.0) - jnp.log1p(jnp.exp(-jnp.abs(z)))) * (1.0 / GATE_TAU)
        la_hi = la.astype(BF16)
        la_lo = (la - la_hi.astype(F32)).astype(BF16)
        cums.append([_dot(tri_ref[...], la_hi[:, d * GLA_KW:(d + 1) * GLA_KW])
                     + _dot(tri_ref[...], la_lo[:, d * GLA_KW:(d + 1) * GLA_KW])
                     for d, tri_ref in enumerate((trif_ref, trib_ref))])
    qks += [wide(r) for r in range(1, len(groups))]
    for r, (rs, qk) in enumerate(zip(groups, qks)):
        q = qk[:, :GLA_KW] * (GLA_DK ** -0.5)
        k = qk[:, GLA_KW:]
        for d, bcs in enumerate(cums[r]):
            cs = slice(d * GLA_KW, (d + 1) * GLA_KW)
            tots = []
            for c in range(CUM_TOKENS // GLA_CHUNK):
                e = c * GLA_CHUNK + (GLA_CHUNK - 1 if d == 0 else 0)
                t = bcs[e:e + 1, :]
                row = r * (CUM_TOKENS // GLA_CHUNK) + c
                tot_ref[0, row:row + 1, cs] = t
                tots.append(jnp.broadcast_to(t, (GLA_CHUNK, GLA_KW)))
            tot = jnp.concatenate(tots, axis=0)
            base = d * QK_COLS
            qk_ref[0, rs, base:base + GLA_KW] = (q * jnp.exp(bcs)).astype(BF16)
            qk_ref[0, rs, base + GLA_KW:base + 2 * GLA_KW] = (k * jnp.exp(-bcs)).astype(BF16)
            qk_ref[0, rs, base + 2 * GLA_KW:base + 3 * GLA_KW] = (k * jnp.exp(tot - bcs)).astype(BF16)


def _front_kernel(*refs, n_batch, alpha, first_layer):
    if first_layer:
        x_ref, pos_ref, ctx_ref = refs[:3]
        refs = refs[3:]
    else:
        h_ref = refs[0]
        refs = refs[1:]
    (mod_ref, w_in_ref, w_out_ref, g_ref, b_ref, wmix_ref, gup_ref, gb_ref, trif_ref, trib_ref,
     o_ref, p_ref, qk_ref, tot_ref, a_ref) = refs
    is_ctx = pl.program_id(0) == n_batch
    groups = [slice(r * FFN_ROWS, (r + 1) * FFN_ROWS) for r in range(o_ref.shape[1] // FFN_ROWS)]
    for rs in groups:
        if first_layer:
            xr = _layernorm(jnp.where(is_ctx, ctx_ref[0, rs, :], x_ref[0, rs, :] + pos_ref[rs, :]))
        else:
            xr = h_ref[0, rs, :]
        o_ref[0, rs, :] = _ffn_rows(xr, mod_ref, 0, w_in_ref, w_out_ref, a_ref, rs, g_ref, b_ref, alpha)
    _inproj_groups(o_ref, groups, mod_ref, wmix_ref, gup_ref, gb_ref, trif_ref, trib_ref, p_ref, qk_ref, tot_ref)


def _layer_block(arr, l):
    nd = arr.ndim
    return pl.BlockSpec((1,) + arr.shape[1:], lambda *_: (l,) + (0,) * (nd - 1), pipeline_mode=pl.Buffered(1))


def _front(src, mod, l, w_in, w_out, ln_g, ln_b, w_mix, gup, gbias, tri_f, tri_b, *, n_batch, alpha):
    first_layer = isinstance(src, tuple)
    B = n_batch
    nseg = B + 1
    tm = FFN_TOKENS
    if first_layer:
        x, pos, ctx_flat = src
        _, L, D = x.shape
        nt = L // tm
        src_specs = [
            pl.BlockSpec((1, tm, D), lambda b, i: (jnp.minimum(b, B - 1), jnp.where(b == B, nt - 1, i), 0)),
            pl.BlockSpec((tm, D), lambda b, i: (jnp.where(b == B, nt - 1, i), 0)),
            pl.BlockSpec((1, tm, D), lambda b, i: (0, jnp.where(b == B, i, 0), 0)),
        ]
    else:
        src = (src,)
        _, L, D = src[0].shape
        nt = L // tm
        src_specs = [pl.BlockSpec((1, tm, D), lambda b, i: (b, i, 0))]
    return pl.pallas_call(
        functools.partial(_front_kernel, n_batch=B, alpha=alpha, first_layer=first_layer),
        grid=(nseg, nt),
        in_specs=src_specs + [
            pl.BlockSpec((1, N_MOD, D), lambda b, i: (b, 0, 0)),
            _layer_block(w_in, l), _layer_block(w_out, l), _layer_block(ln_g, 3 * l), _layer_block(ln_b, 3 * l),
            _layer_block(w_mix, l), _layer_block(gup, l), _layer_block(gbias, l),
            _resident(tri_f.shape), _resident(tri_b.shape),
        ],
        out_specs=[
            pl.BlockSpec((1, tm, D), lambda b, i: (b, i, 0)),
            pl.BlockSpec((1, tm, P_COLS), lambda b, i: (b, i, 0)),
            pl.BlockSpec((1, tm, 2 * QK_COLS), lambda b, i: (b, i, 0)),
            pl.BlockSpec((1, tm // GLA_CHUNK, 2 * GLA_KW), lambda b, i: (b, i, 0)),
        ],
        out_shape=[
            jax.ShapeDtypeStruct((nseg, L, D), F32),
            jax.ShapeDtypeStruct((nseg, L, P_COLS), BF16),
            jax.ShapeDtypeStruct((nseg, L, 2 * QK_COLS), BF16),
            jax.ShapeDtypeStruct((nseg, L // GLA_CHUNK, 2 * GLA_KW), F32),
        ],
        scratch_shapes=[pltpu.VMEM((tm, D_FF), BF16)],
        compiler_params=_params("arbitrary", "arbitrary"),
        name="front",
    )(*src, mod, w_in, w_out, ln_g, ln_b, w_mix, gup, gbias, tri_f, tri_b)


def _gla_block(qk_ref, v_ref, tot_row, st_ref, *, fwd, tokens):
    pair = 2 * GLA_DK
    d = 0 if fwd else 1
    lane = lax.broadcasted_iota(jnp.int32, (pair, pair), 1)
    first_head = lane < GLA_DK
    head_mask = [jnp.where(first_head, 1.0, 0.0).astype(BF16), jnp.where(first_head, 0.0, 1.0).astype(BF16)]
    srow = lax.broadcasted_iota(jnp.int32, (2 * SUB_TOKENS, SUB_TOKENS), 0) % SUB_TOKENS
    scol = lax.broadcasted_iota(jnp.int32, (2 * SUB_TOKENS, SUB_TOKENS), 1)
    keep = ((srow // GLA_CHUNK) == (scol // GLA_CHUNK)) & ((scol <= srow) if fwd else (scol >= srow))

    nchunk = tokens // GLA_CHUNK
    npair = GLA_HEADS // 2
    chunk_rows = [slice(c * GLA_CHUNK, (c + 1) * GLA_CHUNK) for c in range(nchunk)]
    sub_rows = [slice(sb * SUB_TOKENS, (sb + 1) * SUB_TOKENS) for sb in range(tokens // SUB_TOKENS)]

    upd = {(c, p): _dot_tn(v_ref[0, rs, 2 * p * GLA_DV:(2 * p + 2) * GLA_DV],
                           qk_ref[0, rs, 2 * GLA_KW + p * pair:2 * GLA_KW + (p + 1) * pair])
           for c, rs in enumerate(chunk_rows) for p in range(npair)}
    st = [st_ref[p * pair:(p + 1) * pair, :] for p in range(npair)]
    st_start = {}
    for c in (range(nchunk) if fwd else range(nchunk - 1, -1, -1)):
        for p in range(npair):
            st_start[c, p] = st[p].astype(BF16)
            decay = jnp.exp(tot_row(c, slice(d * GLA_KW + p * pair, d * GLA_KW + (p + 1) * pair)))
            u = upd[c, p]
            st[p] = st[p] * decay + jnp.where(first_head, u[:GLA_DV], u[GLA_DV:])
    for p in range(npair):
        st_ref[p * pair:(p + 1) * pair, :] = st[p]

    zeros_half = jnp.zeros((GLA_CHUNK, pair), BF16)
    group_size = len(sub_rows) if fwd else GLA_GROUP
    for g0 in range(0, len(sub_rows), group_size):
        group = list(range(g0, min(g0 + group_size, len(sub_rows))))
        q_h = {}
        scores = {}
        for sb in group:
            rs = sub_rows[sb]
            for p in range(npair):
                qp = qk_ref[0, rs, p * pair:(p + 1) * pair]
                q_h[sb, p] = [qp * head_mask[0], qp * head_mask[1]]
                k_in = qk_ref[0, rs, GLA_KW + p * pair:GLA_KW + (p + 1) * pair]
                scores[sb, p] = _dot_nt(jnp.concatenate(q_h[sb, p], axis=0), k_in)
        inter = {}
        for sb in group:
            for p in range(npair):
                c0 = sb * (SUB_TOKENS // GLA_CHUNK)
                st_cat = jnp.concatenate([st_start[c0, p], st_start[c0 + 1, p]], axis=1)
                q0, q1 = q_h[sb, p]
                q_inter = jnp.concatenate(
                    [jnp.concatenate([q0[:GLA_CHUNK], zeros_half, q1[:GLA_CHUNK], zeros_half], axis=0),
                     jnp.concatenate([zeros_half, q0[GLA_CHUNK:], zeros_half, q1[GLA_CHUNK:]], axis=0)], axis=1)
                inter[sb, p] = _dot_nt(q_inter, st_cat)
        for sb in group:
            rs = sub_rows[sb]
            o_heads = []
            for p in range(npair):
                sc = jnp.where(keep, scores[sb, p], 0.0).astype(BF16)
                for hh in range(2):
                    h = 2 * p + hh
                    hs = slice(hh * SUB_TOKENS, (hh + 1) * SUB_TOKENS)
                    o_heads.append(_dot(sc[hs], v_ref[0, rs, h * GLA_DV:(h + 1) * GLA_DV]) + inter[sb, p][hs])
            yield sb, jnp.concatenate(o_heads, axis=1)


def _gla_kernel(*refs, tokens, nblk, is_ctx):
    if is_ctx:
        qkf_ref, qkb_ref, v_ref, r_ref, tot_ref, ng_ref, o_ref, sfin_ref, of_ref, st_ref = refs
    else:
        qkf_ref, qkb_ref, v_ref, r_ref, tot_ref, s0_ref, ng_ref, o_ref, of_ref, st_ref = refs
    ph = pl.program_id(1)
    j = pl.program_id(2)
    jb = jnp.where(ph == 0, j, nblk - 1 - j)
    base = pl.multiple_of(jb * tokens, tokens)
    nchunk = tokens // GLA_CHUNK

    @pl.when(j == 0)
    def _():
        st_ref[...] = jnp.zeros_like(st_ref) if is_ctx else s0_ref[0, 0]

    if is_ctx:
        odd = (pl.program_id(0) % 2) == 1

        def tot_row(c, ls):
            return jnp.where(odd, tot_ref[0, nchunk + c:nchunk + c + 1, ls], tot_ref[0, c:c + 1, ls])
    else:
        def tot_row(c, ls):
            return tot_ref[0, c:c + 1, ls]

    @pl.when(ph == 0)
    def _():
        for sb, o in _gla_block(qkf_ref, v_ref, tot_row, st_ref, fwd=True, tokens=tokens):
            of_ref[pl.ds(base + sb * SUB_TOKENS, SUB_TOKENS), :] = o

    @pl.when(ph == 1)
    def _():
        def finish(sb, o):
            rs = slice(sb * SUB_TOKENS, (sb + 1) * SUB_TOKENS)
            o = o + of_ref[pl.ds(base + sb * SUB_TOKENS, SUB_TOKENS), :]
            normed = []
            for h in range(GLA_HEADS):
                oh = o[:, h * GLA_DV:(h + 1) * GLA_DV]
                normed.append(oh * lax.rsqrt(jnp.mean(oh * oh, axis=-1, keepdims=True) + LN_EPS))
            on = jnp.concatenate(normed, axis=1) * ng_ref[...]
            o_ref[0, rs, :] = (on * _silu(r_ref[0, rs, :].astype(F32))).astype(BF16)

        pending = None
        for item in _gla_block(qkb_ref, v_ref, tot_row, st_ref, fwd=False, tokens=tokens):
            if pending is not None:
                finish(*pending)
            pending = item
        finish(*pending)

    if is_ctx:
        @pl.when(j == nblk - 1)
        def _():
            sfin_ref[0, 0] = st_ref[...]


def _gla_ctx(p, qk, tot, norm_g, *, n_batch):
    _, L, _ = p.shape
    tb = SEQ_TOKENS
    st_shape = ((GLA_HEADS // 2) * GLA_DV, 2 * GLA_DK)
    return pl.pallas_call(
        functools.partial(_gla_kernel, tokens=tb, nblk=1, is_ctx=True),
        grid=(n_batch, 2, 1),
        in_specs=[
            pl.BlockSpec((1, tb, QK_COLS), lambda b, ph, j: (n_batch, b, 0)),
            pl.BlockSpec((1, tb, QK_COLS), lambda b, ph, j: (n_batch, b, 1)),
            pl.BlockSpec((1, tb, GLA_WIDTH), lambda b, ph, j: (n_batch, b, 1)),
            pl.BlockSpec((1, tb, GLA_WIDTH), lambda b, ph, j: (n_batch, b, 2)),
            pl.BlockSpec((1, 2 * tb // GLA_CHUNK, 2 * GLA_KW), lambda b, ph, j: (n_batch, b // 2, 0)),
            _resident((1, GLA_WIDTH)),
        ],
        out_specs=[
            pl.BlockSpec((1, tb, GLA_WIDTH), lambda b, ph, j: (0, b, 0)),
            pl.BlockSpec((1, 1) + st_shape, lambda b, ph, j: (b, ph, 0, 0)),
        ],
        out_shape=[
            jax.ShapeDtypeStruct((1, L, GLA_WIDTH), BF16),
            jax.ShapeDtypeStruct((n_batch, 2) + st_shape, F32),
        ],
        scratch_shapes=[
            pltpu.VMEM((tb, GLA_WIDTH), F32),
            pltpu.VMEM(st_shape, F32),
        ],
        compiler_params=_params("arbitrary", "arbitrary", "arbitrary"),
        name="gla_ctx",
    )(qk, qk, p, p, tot, norm_g.reshape(1, GLA_WIDTH))


def _gla_latent(p, qk, tot, s0, norm_g, *, n_batch):
    _, L, _ = p.shape
    tb = GLA_TOKENS
    nblk = L // tb
    st_shape = ((GLA_HEADS // 2) * GLA_DV, 2 * GLA_DK)

    def visited(ph, j):
        return jnp.where(ph == 0, j, nblk - 1 - j)

    def fwd_only(cidx):
        return lambda b, ph, j: (b, jnp.where(ph == 0, j, nblk - 1), cidx)

    def bwd_only(cidx):
        return lambda b, ph, j: (b, jnp.where(ph == 0, nblk - 1, nblk - 1 - j), cidx)

    return pl.pallas_call(
        functools.partial(_gla_kernel, tokens=tb, nblk=nblk, is_ctx=False),
        grid=(n_batch, 2, nblk),
        in_specs=[
            pl.BlockSpec((1, tb, QK_COLS), fwd_only(0)),
            pl.BlockSpec((1, tb, QK_COLS), bwd_only(1)),
            pl.BlockSpec((1, tb, GLA_WIDTH), lambda b, ph, j: (b, visited(ph, j), 1)),
            pl.BlockSpec((1, tb, GLA_WIDTH), bwd_only(2)),
            pl.BlockSpec((1, tb // GLA_CHUNK, 2 * GLA_KW), lambda b, ph, j: (b, visited(ph, j), 0)),
            pl.BlockSpec((1, 1) + st_shape, lambda b, ph, j: (b, ph, 0, 0)),
            _resident((1, GLA_WIDTH)),
        ],
        out_specs=pl.BlockSpec((1, tb, GLA_WIDTH), bwd_only(0)),
        out_shape=jax.ShapeDtypeStruct((n_batch, L, GLA_WIDTH), BF16),
        scratch_shapes=[
            pltpu.VMEM((L, GLA_WIDTH), F32),
            pltpu.VMEM(st_shape, F32),
        ],
        compiler_params=_params("arbitrary", "arbitrary", "arbitrary"),
        name="gla_latent",
    )(qk, qk, p, p, tot, s0, norm_g.reshape(1, GLA_WIDTH))


def _back_kernel(h_ref, u_ref, up_ref, un_ref, glat_ref, gctx_ref, mod_ref, band_ref, pw_ref, ps_ref, wmix_ref,
                 g1_ref, b1_ref, w_in_ref, w_out_ref, g2_ref, b2_ref, o_ref, ue_ref, a_ref, *, n_batch, alpha):
    tm = h_ref.shape[1]
    i = pl.program_id(1)
    is_ctx = pl.program_id(0) == n_batch
    first = is_ctx | (i == 0)
    last = is_ctx | (i == pl.num_programs(1) - 1)
    no_halo = jnp.zeros((HALO, POOL_WIDTH), BF16)
    ue_ref[0:HALO, :] = jnp.where(first, no_halo, up_ref[0])
    ue_ref[HALO:HALO + tm, :] = u_ref[0]
    ue_ref[HALO + tm:, :] = jnp.where(last, no_halo, un_ref[0])

    pos = lax.broadcasted_iota(jnp.int32, (POOL_TOKENS, 1), 0)
    nsub = tm // POOL_TOKENS

    groups = [slice(s * POOL_TOKENS, (s + 1) * POOL_TOKENS) for s in range(nsub)]
    wsums = [[_dot(band_ref[0, g], ue_ref[rs.start:rs.stop + 2 * HALO, g * POOL_GROUP:(g + 1) * POOL_GROUP])
              for g in range(len(POOL_WINDOWS))] for rs in groups]
    pool_ys = []
    for s, rs in enumerate(groups):
        starts = first if s == 0 else is_ctx
        ends = last if s == nsub - 1 else is_ctx
        room_lo = jnp.where(starts, pos, POOL_TOKENS)
        room_hi = jnp.where(ends, POOL_TOKENS - 1 - pos, POOL_TOKENS)
        yg = []
        for g, w in enumerate(POOL_WINDOWS):
            cs = slice(g * POOL_GROUP, (g + 1) * POOL_GROUP)
            lo, hi = w // 2, w - 1 - w // 2
            cnt = (jnp.minimum(room_lo, lo) + jnp.minimum(room_hi, hi) + 1).astype(F32)
            pooled = wsums[s][g] / cnt - ue_ref[HALO + rs.start:HALO + rs.stop, cs].astype(F32)
            yg.append(_dot(pooled.astype(BF16), pw_ref[0, g]))
        pool_ys.append((jnp.concatenate(yg, axis=1) * ps_ref[0]).astype(BF16))
    ys = []
    for rs, pool_y in zip(groups, pool_ys):
        gla = jnp.where(is_ctx, gctx_ref[0, rs, :], glat_ref[0, rs, :])
        ys.append(_dot(pool_y, wmix_ref[0, :POOL_WIDTH, :]) + _dot(gla, wmix_ref[0, POOL_WIDTH:, :]))
    for rs, y in zip(groups, ys):
        z = alpha * h_ref[0, rs, :] + mod_ref[0, 5:6, :] * y
        o_ref[0, rs, :] = _layernorm(z) * g1_ref[0] + b1_ref[0]
    for rs in groups:
        o_ref[0, rs, :] = _ffn_rows(o_ref[0, rs, :], mod_ref, 6, w_in_ref, w_out_ref, a_ref, rs, g2_ref, b2_ref,
                                    alpha)


def _pool_bands():
    t = jnp.arange(POOL_TOKENS)[:, None]
    j = jnp.arange(POOL_TOKENS + 2 * HALO)[None, :] - HALO
    inside = (j >= 0) & (j < POOL_TOKENS)
    bands = []
    for w in POOL_WINDOWS:
        lo, hi = w // 2, w - 1 - w // 2
        bands.append((j >= t - lo) & (j <= t + hi))
    bands = jnp.stack(bands)
    return jnp.stack([bands, bands & inside]).astype(BF16)


def _back(h, p, g_lat, g_ctx, mod, l, bands, pool_w, pool_scale, w_mix_out, ln_g, ln_b, w_in, w_out,
          *, n_batch, alpha, nseg):
    _, L, D = h.shape
    tm = BACK_TOKENS
    nt = L // tm
    hb = tm // HALO
    return pl.pallas_call(
        functools.partial(_back_kernel, n_batch=n_batch, alpha=alpha),
        grid=(nseg, nt),
        in_specs=[
            pl.BlockSpec((1, tm, D), lambda b, i: (b, i, 0)),
            pl.BlockSpec((1, tm, POOL_WIDTH), lambda b, i: (b, i, 0)),
            pl.BlockSpec((1, HALO, POOL_WIDTH), lambda b, i: (b, jnp.maximum(i * hb - 1, 0), 0)),
            pl.BlockSpec((1, HALO, POOL_WIDTH), lambda b, i: (b, jnp.minimum((i + 1) * hb, nt * hb - 1), 0)),
            pl.BlockSpec((1, tm, GLA_WIDTH),
                         lambda b, i: (jnp.minimum(b, n_batch - 1), jnp.where(b == n_batch, nt - 1, i), 0)),
            pl.BlockSpec((1, tm, GLA_WIDTH), lambda b, i: (0, jnp.where(b == n_batch, i, 0), 0)),
            pl.BlockSpec((1, N_MOD, D), lambda b, i: (b, 0, 0)),
            pl.BlockSpec((1,) + bands.shape[1:], lambda b, i: (jnp.where(b == n_batch, 1, 0), 0, 0, 0)),
            _layer_block(pool_w, l), _layer_block(pool_scale, l), _layer_block(w_mix_out, l),
            _layer_block(ln_g, 3 * l + 1), _layer_block(ln_b, 3 * l + 1),
            _layer_block(w_in, l), _layer_block(w_out, l),
            _layer_block(ln_g, 3 * l + 2), _layer_block(ln_b, 3 * l + 2),
        ],
        out_specs=pl.BlockSpec((1, tm, D), lambda b, i: (b, i, 0)),
        out_shape=jax.ShapeDtypeStruct((nseg, L, D), F32),
        scratch_shapes=[pltpu.VMEM((tm + 2 * HALO, POOL_WIDTH), BF16), pltpu.VMEM((tm, D_FF), BF16)],
        compiler_params=_params("arbitrary", "arbitrary"),
        name="back",
    )(h, p, p, p, g_lat, g_ctx, mod, bands, pool_w, pool_scale, w_mix_out, ln_g, ln_b, w_in, w_out, ln_g, ln_b)


def _pos_embed_2d(L):
    rows = L // GRID_W
    r = jnp.repeat(jnp.arange(rows, dtype=F32), GRID_W)
    col = jnp.tile(jnp.arange(GRID_W, dtype=F32), rows)
    quarter = D_MODEL // 4
    omega = 1.0 / (10000.0 ** (jnp.arange(quarter, dtype=F32) / quarter))

    def enc(p):
        a = p[:, None] * omega
        return jnp.concatenate([jnp.sin(a), jnp.cos(a)], axis=-1)

    return jnp.concatenate([enc(r), enc(col)], axis=-1)


def kernel(x, c, ctx, c_ctx, w_ada, b_ada, ln_g, ln_b, ffa_w_in, ffa_w_out, mix_w_in, pool_w, pool_scale,
           gate_up_f, gate_bias_f, gate_up_b, gate_bias_b, gla_norm_g, mix_w_out, ffb_w_in, ffb_w_out):
    B, L, D = x.shape
    LC = ctx.shape[1]
    depth = w_ada.shape[0]
    assert D == D_MODEL and LC == SEQ_TOKENS == POOL_TOKENS == FFN_ROWS == CUM_TOKENS and B * LC == L
    assert L % FFN_TOKENS == 0 and L % BACK_TOKENS == 0 and L % GLA_TOKENS == 0
    alpha = (2.0 * depth) ** 0.25
    nseg = B + 1

    rows = -(-nseg // 8) * 8
    cond = jnp.concatenate([c, c_ctx[None, :], jnp.zeros((rows - nseg, D), F32)], axis=0)
    mod = _ada(cond, w_ada, b_ada)[:, :nseg].reshape(depth, nseg, N_MOD, D)

    zeros_gd = jnp.zeros((depth, D, GD_PAD - 2 * GATE_RANK), F32)
    w_mix_in = jnp.concatenate([mix_w_in, zeros_gd], axis=-1).astype(BF16)
    gup = jnp.zeros((depth, GD_PAD, 2 * GLA_KW), F32)
    gup = gup.at[:, :GATE_RANK, :GLA_KW].set(gate_up_f)
    gup = gup.at[:, GATE_RANK:2 * GATE_RANK, GLA_KW:].set(gate_up_b).astype(BF16)
    gbias = jnp.concatenate([gate_bias_f, gate_bias_b], axis=-1).reshape(depth, 1, 2 * GLA_KW)
    ffa_in, ffa_out = ffa_w_in.astype(BF16), ffa_w_out.astype(BF16)
    ffb_in, ffb_out = ffb_w_in.astype(BF16), ffb_w_out.astype(BF16)
    w_mix_out = mix_w_out.astype(BF16)
    pw = pool_w.astype(BF16)
    ps = pool_scale.reshape(depth, 1, POOL_WIDTH)
    lng = ln_g.reshape(depth * 3, 1, D)
    lnb = ln_b.reshape(depth * 3, 1, D)
    ti = jnp.arange(CUM_TOKENS)
    same_chunk = (ti[:, None] // GLA_CHUNK) == (ti[None, :] // GLA_CHUNK)
    tri_f = (same_chunk & (ti[None, :] <= ti[:, None])).astype(BF16)
    tri_b = (same_chunk & (ti[None, :] >= ti[:, None])).astype(BF16)
    bands = _pool_bands()

    h = (x, _pos_embed_2d(L), ctx.reshape(1, L, D))
    for l in range(depth):
        last = l == depth - 1
        h, p, qk, tot = _front(h, mod[l], l, ffa_in, ffa_out, lng, lnb, w_mix_in, gup, gbias, tri_f, tri_b,
                               n_batch=B, alpha=alpha)
        g_ctx, s_ctx = _gla_ctx(p, qk, tot, gla_norm_g[l], n_batch=B)
        g_lat = _gla_latent(p, qk, tot, s_ctx, gla_norm_g[l], n_batch=B)
        h = _back(h, p, g_lat, g_ctx, mod[l], l, bands, pw, ps, w_mix_out, lng, lnb, ffb_in, ffb_out,
                  n_batch=B, alpha=alpha, nseg=B if last else nseg)
    return h
```

```python
import functools

import jax
import jax.numpy as jnp
from jax import lax
from jax.experimental import pallas as pl
from jax.experimental.pallas import tpu as pltpu

F32 = jnp.float32
BF16 = jnp.bfloat16

D_MODEL = 1024
D_FF = 2816
N_MOD = 9
POOL_WINDOWS = (2, 4, 8, 16)
POOL_GROUP = 128
POOL_WIDTH = POOL_GROUP * len(POOL_WINDOWS)
GLA_HEADS = 4
GLA_DK = 64
GLA_DV = 128
GLA_KW = GLA_HEADS * GLA_DK
GLA_WIDTH = GLA_HEADS * GLA_DV
GATE_RANK = 16
GATE_TAU = 16.0
GLA_CHUNK = 64
GRID_W = 64
LN_EPS = 1e-6
MAIN_COLS = POOL_WIDTH + 2 * GLA_KW + 2 * GLA_WIDTH
P_COLS = POOL_WIDTH + 2 * GLA_WIDTH
QK_COLS = 3 * GLA_KW
GD_PAD = 256

VMEM_LIMIT_BYTES = 56 * 1024 * 1024
FFN_TOKENS = 512
BACK_TOKENS = 1024
FFN_CHUNK = 256
FFN_ROWS = 256
SEQ_TOKENS = 256
GLA_TOKENS = 1024
SUB_TOKENS = 128
GLA_GROUP = 2
CUM_TOKENS = 256
POOL_TOKENS = 256
HALO = 16


def _dot(a, b):
    return jnp.dot(a, b, preferred_element_type=F32)


def _dot_nt(a, b):
    return lax.dot_general(a, b, (((1,), (1,)), ((), ())), preferred_element_type=F32)


def _dot_tn(a, b):
    return lax.dot_general(a, b, (((0,), (0,)), ((), ())), preferred_element_type=F32)


def _silu(x):
    return x / (1.0 + jnp.exp(-x))


def _layernorm(z):
    mu = jnp.mean(z, axis=-1, keepdims=True)
    zc = z - mu
    var = jnp.mean(zc * zc, axis=-1, keepdims=True)
    return zc * lax.rsqrt(var + LN_EPS)


def _params(*sem):
    return pltpu.CompilerParams(dimension_semantics=sem, vmem_limit_bytes=VMEM_LIMIT_BYTES)


def _resident(shape):
    nd = len(shape)
    return pl.BlockSpec(shape, lambda *_: (0,) * nd, pipeline_mode=pl.Buffered(1))


def _ada_kernel(c_ref, w_ref, b_ref, o_ref):
    s = _silu(c_ref[...])
    o_ref[0] = _dot(s.astype(BF16), w_ref[0].astype(BF16)) + b_ref[0]


def _ada(cond, w_ada, b_ada):
    depth, d, n = w_ada.shape
    rows = cond.shape[0]
    tn = 1024
    return pl.pallas_call(
        _ada_kernel,
        grid=(depth, n // tn),
        in_specs=[
            pl.BlockSpec((rows, d), lambda l, j: (0, 0)),
            pl.BlockSpec((1, d, tn), lambda l, j: (l, 0, j)),
            pl.BlockSpec((1, 1, tn), lambda l, j: (l, 0, j)),
        ],
        out_specs=pl.BlockSpec((1, rows, tn), lambda l, j: (l, 0, j)),
        out_shape=jax.ShapeDtypeStruct((depth, rows, n), F32),
        compiler_params=_params("arbitrary", "arbitrary"),
        name="ada",
    )(cond, w_ada, b_ada.reshape(depth, 1, n))


def _ffn_rows(xr, mod_ref, k0, w_in_ref, w_out_ref, a_ref, rs, g_ref, b_ref, alpha):
    shift = mod_ref[0, k0:k0 + 1, :]
    scale = mod_ref[0, k0 + 1:k0 + 2, :]
    gate = mod_ref[0, k0 + 2:k0 + 3, :]
    xm = (xr * (1.0 + scale) + shift).astype(BF16)
    for j in range(D_FF // FFN_CHUNK):
        lo = j * FFN_CHUNK
        g = _dot(xm, w_in_ref[0, :, lo:lo + FFN_CHUNK])
        u = _dot(xm, w_in_ref[0, :, D_FF + lo:D_FF + lo + FFN_CHUNK])
        a_ref[rs, lo:lo + FFN_CHUNK] = (_silu(g) * u).astype(BF16)
    y = _dot(a_ref[rs, :], w_out_ref[0])
    z = alpha * xr + (0.5 * gate) * y
    return _layernorm(z) * g_ref[0] + b_ref[0]


def _inproj_groups(h_ref, groups, mod_ref, w_ref, gup_ref, gb_ref, trif_ref, trib_ref, p_ref, qk_ref, tot_ref):
    shift = mod_ref[0, 3:4, :]
    scale = mod_ref[0, 4:5, :]
    xms = [(h_ref[0, rs, :] * (1.0 + scale) + shift).astype(BF16) for rs in groups]
    gds = [_dot(xm, w_ref[0, :, MAIN_COLS:MAIN_COLS + GD_PAD]) for xm in xms]
    zs = [_dot(gd.astype(BF16), gup_ref[0]) + gb_ref[0] for gd in gds]

    def wide(r):
        rs, xm = groups[r], xms[r]
        qk = _dot(xm, w_ref[0, :, POOL_WIDTH:POOL_WIDTH + 2 * GLA_KW])
        p_ref[0, rs, 0:POOL_WIDTH] = _dot(xm, w_ref[0, :, 0:POOL_WIDTH]).astype(BF16)
        for j in (1, 2):
            p_ref[0, rs, j * 512:(j + 1) * 512] = _dot(xm, w_ref[0, :, 512 + j * 512:1024 + j * 512]).astype(BF16)
        return qk

    qks = [wide(0)]
    cums = []
    for z in zs:
        la = (jnp.minimum(z, 0.0) - jnp.log1p(jnp.exp(-jnp.abs(z)))) * (1.0 / GATE_TAU)
        la_hi = la.astype(BF16)
        la_lo = (la - la_hi.astype(F32)).astype(BF16)
        cums.append([_dot(tri_ref[...], la_hi[:, d * GLA_KW:(d + 1) * GLA_KW])
                     + _dot(tri_ref[...], la_lo[:, d * GLA_KW:(d + 1) * GLA_KW])
                     for d, tri_ref in enumerate((trif_ref, trib_ref))])
    qks += [wide(r) for r in range(1, len(groups))]
    for r, (rs, qk) in enumerate(zip(groups, qks)):
        q = qk[:, :GLA_KW] * (GLA_DK ** -0.5)
        k = qk[:, GLA_KW:]
        for d, bcs in enumerate(cums[r]):
            cs = slice(d * GLA_KW, (d + 1) * GLA_KW)
            tots = []
            for c in range(CUM_TOKENS // GLA_CHUNK):
                e = c * GLA_CHUNK + (GLA_CHUNK - 1 if d == 0 else 0)
                t = bcs[e:e + 1, :]
                row = r * (CUM_TOKENS // GLA_CHUNK) + c
                tot_ref[0, row:row + 1, cs] = t
                tots.append(jnp.broadcast_to(t, (GLA_CHUNK, GLA_KW)))
            tot = jnp.concatenate(tots, axis=0)
            base = d * QK_COLS
            qk_ref[0, rs, base:base + GLA_KW] = (q * jnp.exp(bcs)).astype(BF16)
            qk_ref[0, rs, base + GLA_KW:base + 2 * GLA_KW] = (k * jnp.exp(-bcs)).astype(BF16)
            qk_ref[0, rs, base + 2 * GLA_KW:base + 3 * GLA_KW] = (k * jnp.exp(tot - bcs)).astype(BF16)


def _front_kernel(*refs, n_batch, alpha, first_layer):
    if first_layer:
        x_ref, pos_ref, ctx_ref = refs[:3]
        refs = refs[3:]
    else:
        h_ref = refs[0]
        refs = refs[1:]
    (mod_ref, w_in_ref, w_out_ref, g_ref, b_ref, wmix_ref, gup_ref, gb_ref, trif_ref, trib_ref,
     o_ref, p_ref, qk_ref, tot_ref, a_ref) = refs
    is_ctx = pl.program_id(0) == n_batch
    groups = [slice(r * FFN_ROWS, (r + 1) * FFN_ROWS) for r in range(o_ref.shape[1] // FFN_ROWS)]
    for rs in groups:
        if first_layer:
            xr = _layernorm(jnp.where(is_ctx, ctx_ref[0, rs, :], x_ref[0, rs, :] + pos_ref[rs, :]))
        else:
            xr = h_ref[0, rs, :]
        o_ref[0, rs, :] = _ffn_rows(xr, mod_ref, 0, w_in_ref, w_out_ref, a_ref, rs, g_ref, b_ref, alpha)
    _inproj_groups(o_ref, groups, mod_ref, wmix_ref, gup_ref, gb_ref, trif_ref, trib_ref, p_ref, qk_ref, tot_ref)


def _layer_block(arr, l):
    nd = arr.ndim
    return pl.BlockSpec((1,) + arr.shape[1:], lambda *_: (l,) + (0,) * (nd - 1), pipeline_mode=pl.Buffered(1))


def _front(src, mod, l, w_in, w_out, ln_g, ln_b, w_mix, gup, gbias, tri_f, tri_b, *, n_batch, alpha):
    first_layer = isinstance(src, tuple)
    B = n_batch
    nseg = B + 1
    tm = FFN_TOKENS
    if first_layer:
        x, pos, ctx_flat = src
        _, L, D = x.shape
        nt = L // tm
        src_specs = [
            pl.BlockSpec((1, tm, D), lambda b, i: (jnp.minimum(b, B - 1), jnp.where(b == B, nt - 1, i), 0)),
            pl.BlockSpec((tm, D), lambda b, i: (jnp.where(b == B, nt - 1, i), 0)),
            pl.BlockSpec((1, tm, D), lambda b, i: (0, jnp.where(b == B, i, 0), 0)),
        ]
    else:
        src = (src,)
        _, L, D = src[0].shape
        nt = L // tm
        src_specs = [pl.BlockSpec((1, tm, D), lambda b, i: (b, i, 0))]
    return pl.pallas_call(
        functools.partial(_front_kernel, n_batch=B, alpha=alpha, first_layer=first_layer),
        grid=(nseg, nt),
        in_specs=src_specs + [
            pl.BlockSpec((1, N_MOD, D), lambda b, i: (b, 0, 0)),
            _layer_block(w_in, l), _layer_block(w_out, l), _layer_block(ln_g, 3 * l), _layer_block(ln_b, 3 * l),
            _layer_block(w_mix, l), _layer_block(gup, l), _layer_block(gbias, l),
            _resident(tri_f.shape), _resident(tri_b.shape),
        ],
        out_specs=[
            pl.BlockSpec((1, tm, D), lambda b, i: (b, i, 0)),
            pl.BlockSpec((1, tm, P_COLS), lambda b, i: (b, i, 0)),
            pl.BlockSpec((1, tm, 2 * QK_COLS), lambda b, i: (b, i, 0)),
            pl.BlockSpec((1, tm // GLA_CHUNK, 2 * GLA_KW), lambda b, i: (b, i, 0)),
        ],
        out_shape=[
            jax.ShapeDtypeStruct((nseg, L, D), F32),
            jax.ShapeDtypeStruct((nseg, L, P_COLS), BF16),
            jax.ShapeDtypeStruct((nseg, L, 2 * QK_COLS), BF16),
            jax.ShapeDtypeStruct((nseg, L // GLA_CHUNK, 2 * GLA_KW), F32),
        ],
        scratch_shapes=[pltpu.VMEM((tm, D_FF), BF16)],
        compiler_params=_params("arbitrary", "arbitrary"),
        name="front",
    )(*src, mod, w_in, w_out, ln_g, ln_b, w_mix, gup, gbias, tri_f, tri_b)


def _gla_block(qk_ref, v_ref, tot_row, st_ref, *, fwd, tokens):
    pair = 2 * GLA_DK
    d = 0 if fwd else 1
    lane = lax.broadcasted_iota(jnp.int32, (pair, pair), 1)
    first_head = lane < GLA_DK
    head_mask = [jnp.where(first_head, 1.0, 0.0).astype(BF16), jnp.where(first_head, 0.0, 1.0).astype(BF16)]
    srow = lax.broadcasted_iota(jnp.int32, (2 * SUB_TOKENS, SUB_TOKENS), 0) % SUB_TOKENS
    scol = lax.broadcasted_iota(jnp.int32, (2 * SUB_TOKENS, SUB_TOKENS), 1)
    keep = ((srow // GLA_CHUNK) == (scol // GLA_CHUNK)) & ((scol <= srow) if fwd else (scol >= srow))

    nchunk = tokens // GLA_CHUNK
    npair = GLA_HEADS // 2
    chunk_rows = [slice(c * GLA_CHUNK, (c + 1) * GLA_CHUNK) for c in range(nchunk)]
    sub_rows = [slice(sb * SUB_TOKENS, (sb + 1) * SUB_TOKENS) for sb in range(tokens // SUB_TOKENS)]

    upd = {(c, p): _dot_tn(v_ref[0, rs, 2 * p * GLA_DV:(2 * p + 2) * GLA_DV],
                           qk_ref[0, rs, 2 * GLA_KW + p * pair:2 * GLA_KW + (p + 1) * pair])
           for c, rs in enumerate(chunk_rows) for p in range(npair)}
    st = [st_ref[p * pair:(p + 1) * pair, :] for p in range(npair)]
    st_start = {}
    for c in (range(nchunk) if fwd else range(nchunk - 1, -1, -1)):
        for p in range(npair):
            st_start[c, p] = st[p].astype(BF16)
            decay = jnp.exp(tot_row(c, slice(d * GLA_KW + p * pair, d * GLA_KW + (p + 1) * pair)))
            u = upd[c, p]
            st[p] = st[p] * decay + jnp.where(first_head, u[:GLA_DV], u[GLA_DV:])
    for p in range(npair):
        st_ref[p * pair:(p + 1) * pair, :] = st[p]

    zeros_half = jnp.zeros((GLA_CHUNK, pair), BF16)
    group_size = len(sub_rows) if fwd else GLA_GROUP
    for g0 in range(0, len(sub_rows), group_size):
        group = list(range(g0, min(g0 + group_size, len(sub_rows))))
        q_h = {}
        scores = {}
        for sb in group:
            rs = sub_rows[sb]
            for p in range(npair):
                qp = qk_ref[0, rs, p * pair:(p + 1) * pair]
                q_h[sb, p] = [qp * head_mask[0], qp * head_mask[1]]
                k_in = qk_ref[0, rs, GLA_KW + p * pair:GLA_KW + (p + 1) * pair]
                scores[sb, p] = _dot_nt(jnp.concatenate(q_h[sb, p], axis=0), k_in)
        inter = {}
        for sb in group:
            for p in range(npair):
                c0 = sb * (SUB_TOKENS // GLA_CHUNK)
                st_cat = jnp.concatenate([st_start[c0, p], st_start[c0 + 1, p]], axis=1)
                q0, q1 = q_h[sb, p]
                q_inter = jnp.concatenate(
                    [jnp.concatenate([q0[:GLA_CHUNK], zeros_half, q1[:GLA_CHUNK], zeros_half], axis=0),
                     jnp.concatenate([zeros_half, q0[GLA_CHUNK:], zeros_half, q1[GLA_CHUNK:]], axis=0)], axis=1)
                inter[sb, p] = _dot_nt(q_inter, st_cat)
        for sb in group:
            rs = sub_rows[sb]
            o_heads = []
            for p in range(npair):
                sc = jnp.where(keep, scores[sb, p], 0.0).astype(BF16)
                for hh in range(2):
                    h = 2 * p + hh
                    hs = slice(hh * SUB_TOKENS, (hh + 1) * SUB_TOKENS)
                    o_heads.append(_dot(sc[hs], v_ref[0, rs, h * GLA_DV:(h + 1) * GLA_DV]) + inter[sb, p][hs])
            yield sb, jnp.concatenate(o_heads, axis=1)


def _gla_kernel(*refs, tokens, nblk, is_ctx):
    if is_ctx:
        qkf_ref, qkb_ref, v_ref, r_ref, tot_ref, ng_ref, o_ref, sfin_ref, of_ref, st_ref = refs
    else:
        qkf_ref, qkb_ref, v_ref, r_ref, tot_ref, s0_ref, ng_ref, o_ref, of_ref, st_ref = refs
    ph = pl.program_id(1)
    j = pl.program_id(2)
    jb = jnp.where(ph == 0, j, nblk - 1 - j)
    base = pl.multiple_of(jb * tokens, tokens)
    nchunk = tokens // GLA_CHUNK

    @pl.when(j == 0)
    def _():
        st_ref[...] = jnp.zeros_like(st_ref) if is_ctx else s0_ref[0, 0]

    if is_ctx:
        odd = (pl.program_id(0) % 2) == 1

        def tot_row(c, ls):
            return jnp.where(odd, tot_ref[0, nchunk + c:nchunk + c + 1, ls], tot_ref[0, c:c + 1, ls])
    else:
        def tot_row(c, ls):
            return tot_ref[0, c:c + 1, ls]

    @pl.when(ph == 0)
    def _():
        for sb, o in _gla_block(qkf_ref, v_ref, tot_row, st_ref, fwd=True, tokens=tokens):
            of_ref[pl.ds(base + sb * SUB_TOKENS, SUB_TOKENS), :] = o

    @pl.when(ph == 1)
    def _():
        for sb, o in _gla_block(qkb_ref, v_ref, tot_row, st_ref, fwd=False, tokens=tokens):
            rs = slice(sb * SUB_TOKENS, (sb + 1) * SUB_TOKENS)
            o = o + of_ref[pl.ds(base + sb * SUB_TOKENS, SUB_TOKENS), :]
            normed = []
            for h in range(GLA_HEADS):
                oh = o[:, h * GLA_DV:(h + 1) * GLA_DV]
                normed.append(oh * lax.rsqrt(jnp.mean(oh * oh, axis=-1, keepdims=True) + LN_EPS))
            on = jnp.concatenate(normed, axis=1) * ng_ref[...]
            o_ref[0, rs, :] = (on * _silu(r_ref[0, rs, :].astype(F32))).astype(BF16)

    if is_ctx:
        @pl.when(j == nblk - 1)
        def _():
            sfin_ref[0, 0] = st_ref[...]


def _gla_ctx(p, qk, tot, norm_g, *, n_batch):
    _, L, _ = p.shape
    tb = SEQ_TOKENS
    st_shape = ((GLA_HEADS // 2) * GLA_DV, 2 * GLA_DK)
    return pl.pallas_call(
        functools.partial(_gla_kernel, tokens=tb, nblk=1, is_ctx=True),
        grid=(n_batch, 2, 1),
        in_specs=[
            pl.BlockSpec((1, tb, QK_COLS), lambda b, ph, j: (n_batch, b, 0)),
            pl.BlockSpec((1, tb, QK_COLS), lambda b, ph, j: (n_batch, b, 1)),
            pl.BlockSpec((1, tb, GLA_WIDTH), lambda b, ph, j: (n_batch, b, 1)),
            pl.BlockSpec((1, tb, GLA_WIDTH), lambda b, ph, j: (n_batch, b, 2)),
            pl.BlockSpec((1, 2 * tb // GLA_CHUNK, 2 * GLA_KW), lambda b, ph, j: (n_batch, b // 2, 0)),
            _resident((1, GLA_WIDTH)),
        ],
        out_specs=[
            pl.BlockSpec((1, tb, GLA_WIDTH), lambda b, ph, j: (0, b, 0)),
            pl.BlockSpec((1, 1) + st_shape, lambda b, ph, j: (b, ph, 0, 0)),
        ],
        out_shape=[
            jax.ShapeDtypeStruct((1, L, GLA_WIDTH), BF16),
            jax.ShapeDtypeStruct((n_batch, 2) + st_shape, F32),
        ],
        scratch_shapes=[
            pltpu.VMEM((tb, GLA_WIDTH), F32),
            pltpu.VMEM(st_shape, F32),
        ],
        compiler_params=_params("arbitrary", "arbitrary", "arbitrary"),
        name="gla_ctx",
    )(qk, qk, p, p, tot, norm_g.reshape(1, GLA_WIDTH))


def _gla_latent(p, qk, tot, s0, norm_g, *, n_batch):
    _, L, _ = p.shape
    tb = GLA_TOKENS
    nblk = L // tb
    st_shape = ((GLA_HEADS // 2) * GLA_DV, 2 * GLA_DK)

    def visited(ph, j):
        return jnp.where(ph == 0, j, nblk - 1 - j)

    def fwd_only(cidx):
        return lambda b, ph, j: (b, jnp.where(ph == 0, j, nblk - 1), cidx)

    def bwd_only(cidx):
        return lambda b, ph, j: (b, jnp.where(ph == 0, nblk - 1, nblk - 1 - j), cidx)

    return pl.pallas_call(
        functools.partial(_gla_kernel, tokens=tb, nblk=nblk, is_ctx=False),
        grid=(n_batch, 2, nblk),
        in_specs=[
            pl.BlockSpec((1, tb, QK_COLS), fwd_only(0)),
            pl.BlockSpec((1, tb, QK_COLS), bwd_only(1)),
            pl.BlockSpec((1, tb, GLA_WIDTH), lambda b, ph, j: (b, visited(ph, j), 1)),
            pl.BlockSpec((1, tb, GLA_WIDTH), bwd_only(2)),
            pl.BlockSpec((1, tb // GLA_CHUNK, 2 * GLA_KW), lambda b, ph, j: (b, visited(ph, j), 0)),
            pl.BlockSpec((1, 1) + st_shape, lambda b, ph, j: (b, ph, 0, 0)),
            _resident((1, GLA_WIDTH)),
        ],
        out_specs=pl.BlockSpec((1, tb, GLA_WIDTH), bwd_only(0)),
        out_shape=jax.ShapeDtypeStruct((n_batch, L, GLA_WIDTH), BF16),
        scratch_shapes=[
            pltpu.VMEM((L, GLA_WIDTH), F32),
            pltpu.VMEM(st_shape, F32),
        ],
        compiler_params=_params("arbitrary", "arbitrary", "arbitrary"),
        name="gla_latent",
    )(qk, qk, p, p, tot, s0, norm_g.reshape(1, GLA_WIDTH))


def _back_kernel(h_ref, u_ref, up_ref, un_ref, glat_ref, gctx_ref, mod_ref, band_ref, pw_ref, ps_ref, wmix_ref,
                 g1_ref, b1_ref, w_in_ref, w_out_ref, g2_ref, b2_ref, o_ref, ue_ref, a_ref, *, n_batch, alpha):
    tm = h_ref.shape[1]
    i = pl.program_id(1)
    is_ctx = pl.program_id(0) == n_batch
    first = is_ctx | (i == 0)
    last = is_ctx | (i == pl.num_programs(1) - 1)
    no_halo = jnp.zeros((HALO, POOL_WIDTH), BF16)
    ue_ref[0:HALO, :] = jnp.where(first, no_halo, up_ref[0])
    ue_ref[HALO:HALO + tm, :] = u_ref[0]
    ue_ref[HALO + tm:, :] = jnp.where(last, no_halo, un_ref[0])

    pos = lax.broadcasted_iota(jnp.int32, (POOL_TOKENS, 1), 0)
    nsub = tm // POOL_TOKENS

    groups = [slice(s * POOL_TOKENS, (s + 1) * POOL_TOKENS) for s in range(nsub)]
    wsums = [[_dot(band_ref[0, g], ue_ref[rs.start:rs.stop + 2 * HALO, g * POOL_GROUP:(g + 1) * POOL_GROUP])
              for g in range(len(POOL_WINDOWS))] for rs in groups]
    pool_ys = []
    for s, rs in enumerate(groups):
        starts = first if s == 0 else is_ctx
        ends = last if s == nsub - 1 else is_ctx
        room_lo = jnp.where(starts, pos, POOL_TOKENS)
        room_hi = jnp.where(ends, POOL_TOKENS - 1 - pos, POOL_TOKENS)
        yg = []
        for g, w in enumerate(POOL_WINDOWS):
            cs = slice(g * POOL_GROUP, (g + 1) * POOL_GROUP)
            lo, hi = w // 2, w - 1 - w // 2
            cnt = (jnp.minimum(room_lo, lo) + jnp.minimum(room_hi, hi) + 1).astype(F32)
            pooled = wsums[s][g] / cnt - ue_ref[HALO + rs.start:HALO + rs.stop, cs].astype(F32)
            yg.append(_dot(pooled.astype(BF16), pw_ref[0, g]))
        pool_ys.append((jnp.concatenate(yg, axis=1) * ps_ref[0]).astype(BF16))
    ys = []
    for rs, pool_y in zip(groups, pool_ys):
        gla = jnp.where(is_ctx, gctx_ref[0, rs, :], glat_ref[0, rs, :])
        ys.append(_dot(pool_y, wmix_ref[0, :POOL_WIDTH, :]) + _dot(gla, wmix_ref[0, POOL_WIDTH:, :]))
    for rs, y in zip(groups, ys):
        z = alpha * h_ref[0, rs, :] + mod_ref[0, 5:6, :] * y
        o_ref[0, rs, :] = _layernorm(z) * g1_ref[0] + b1_ref[0]
    for rs in groups:
        o_ref[0, rs, :] = _ffn_rows(o_ref[0, rs, :], mod_ref, 6, w_in_ref, w_out_ref, a_ref, rs, g2_ref, b2_ref,
                                    alpha)


def _pool_bands():
    t = jnp.arange(POOL_TOKENS)[:, None]
    j = jnp.arange(POOL_TOKENS + 2 * HALO)[None, :] - HALO
    inside = (j >= 0) & (j < POOL_TOKENS)
    bands = []
    for w in POOL_WINDOWS:
        lo, hi = w // 2, w - 1 - w // 2
        bands.append((j >= t - lo) & (j <= t + hi))
    bands = jnp.stack(bands)
    return jnp.stack([bands, bands & inside]).astype(BF16)


def _back(h, p, g_lat, g_ctx, mod, l, bands, pool_w, pool_scale, w_mix_out, ln_g, ln_b, w_in, w_out,
          *, n_batch, alpha, nseg):
    _, L, D = h.shape
    tm = BACK_TOKENS
    nt = L // tm
    hb = tm // HALO
    return pl.pallas_call(
        functools.partial(_back_kernel, n_batch=n_batch, alpha=alpha),
        grid=(nseg, nt),
        in_specs=[
            pl.BlockSpec((1, tm, D), lambda b, i: (b, i, 0)),
            pl.BlockSpec((1, tm, POOL_WIDTH), lambda b, i: (b, i, 0)),
            pl.BlockSpec((1, HALO, POOL_WIDTH), lambda b, i: (b, jnp.maximum(i * hb - 1, 0), 0)),
            pl.BlockSpec((1, HALO, POOL_WIDTH), lambda b, i: (b, jnp.minimum((i + 1) * hb, nt * hb - 1), 0)),
            pl.BlockSpec((1, tm, GLA_WIDTH),
                         lambda b, i: (jnp.minimum(b, n_batch - 1), jnp.where(b == n_batch, nt - 1, i), 0)),
            pl.BlockSpec((1, tm, GLA_WIDTH), lambda b, i: (0, jnp.where(b == n_batch, i, 0), 0)),
            pl.BlockSpec((1, N_MOD, D), lambda b, i: (b, 0, 0)),
            pl.BlockSpec((1,) + bands.shape[1:], lambda b, i: (jnp.where(b == n_batch, 1, 0), 0, 0, 0)),
            _layer_block(pool_w, l), _layer_block(pool_scale, l), _layer_block(w_mix_out, l),
            _layer_block(ln_g, 3 * l + 1), _layer_block(ln_b, 3 * l + 1),
            _layer_block(w_in, l), _layer_block(w_out, l),
            _layer_block(ln_g, 3 * l + 2), _layer_block(ln_b, 3 * l + 2),
        ],
        out_specs=pl.BlockSpec((1, tm, D), lambda b, i: (b, i, 0)),
        out_shape=jax.ShapeDtypeStruct((nseg, L, D), F32),
        scratch_shapes=[pltpu.VMEM((tm + 2 * HALO, POOL_WIDTH), BF16), pltpu.VMEM((tm, D_FF), BF16)],
        compiler_params=_params("arbitrary", "arbitrary"),
        name="back",
    )(h, p, p, p, g_lat, g_ctx, mod, bands, pool_w, pool_scale, w_mix_out, ln_g, ln_b, w_in, w_out, ln_g, ln_b)


def _pos_embed_2d(L):
    rows = L // GRID_W
    quarter = D_MODEL // 4
    omega = 1.0 / (10000.0 ** (jnp.arange(quarter, dtype=F32) / quarter))

    def enc(n):
        a = jnp.arange(n, dtype=F32)[:, None] * omega
        return jnp.concatenate([jnp.sin(a), jnp.cos(a)], axis=-1)

    return jnp.concatenate([jnp.repeat(enc(rows), GRID_W, axis=0), jnp.tile(enc(GRID_W), (rows, 1))], axis=-1)


def kernel(x, c, ctx, c_ctx, w_ada, b_ada, ln_g, ln_b, ffa_w_in, ffa_w_out, mix_w_in, pool_w, pool_scale,
           gate_up_f, gate_bias_f, gate_up_b, gate_bias_b, gla_norm_g, mix_w_out, ffb_w_in, ffb_w_out):
    B, L, D = x.shape
    LC = ctx.shape[1]
    depth = w_ada.shape[0]
    assert D == D_MODEL and LC == SEQ_TOKENS == POOL_TOKENS == FFN_ROWS == CUM_TOKENS and B * LC == L
    assert L % FFN_TOKENS == 0 and L % BACK_TOKENS == 0 and L % GLA_TOKENS == 0
    alpha = (2.0 * depth) ** 0.25
    nseg = B + 1

    rows = -(-nseg // 8) * 8
    cond = jnp.concatenate([c, c_ctx[None, :], jnp.zeros((rows - nseg, D), F32)], axis=0)
    mod = _ada(cond, w_ada, b_ada)[:, :nseg].reshape(depth, nseg, N_MOD, D)

    zeros_gd = jnp.zeros((depth, D, GD_PAD - 2 * GATE_RANK), BF16)
    w_mix_in = jnp.concatenate([mix_w_in.astype(BF16), zeros_gd], axis=-1)
    gup = jnp.zeros((depth, GD_PAD, 2 * GLA_KW), F32)
    gup = gup.at[:, :GATE_RANK, :GLA_KW].set(gate_up_f)
    gup = gup.at[:, GATE_RANK:2 * GATE_RANK, GLA_KW:].set(gate_up_b).astype(BF16)
    gbias = jnp.concatenate([gate_bias_f, gate_bias_b], axis=-1).reshape(depth, 1, 2 * GLA_KW)
    ffa_in, ffa_out = ffa_w_in.astype(BF16), ffa_w_out.astype(BF16)
    ffb_in, ffb_out = ffb_w_in.astype(BF16), ffb_w_out.astype(BF16)
    w_mix_out = mix_w_out.astype(BF16)
    pw = pool_w.astype(BF16)
    ps = pool_scale.reshape(depth, 1, POOL_WIDTH)
    lng = ln_g.reshape(depth * 3, 1, D)
    lnb = ln_b.reshape(depth * 3, 1, D)
    ti = jnp.arange(CUM_TOKENS)
    same_chunk = (ti[:, None] // GLA_CHUNK) == (ti[None, :] // GLA_CHUNK)
    tri_f = (same_chunk & (ti[None, :] <= ti[:, None])).astype(BF16)
    tri_b = (same_chunk & (ti[None, :] >= ti[:, None])).astype(BF16)
    bands = _pool_bands()

    h = (x, _pos_embed_2d(L), ctx.reshape(1, L, D))
    for l in range(depth):
        last = l == depth - 1
        h, p, qk, tot = _front(h, mod[l], l, ffa_in, ffa_out, lng, lnb, w_mix_in, gup, gbias, tri_f, tri_b,
                               n_batch=B, alpha=alpha)
        g_ctx, s_ctx = _gla_ctx(p, qk, tot, gla_norm_g[l], n_batch=B)
        g_lat = _gla_latent(p, qk, tot, s_ctx, gla_norm_g[l], n_batch=B)
        h = _back(h, p, g_lat, g_ctx, mod[l], l, bands, pw, ps, w_mix_out, lng, lnb, ffb_in, ffb_out,
                  n_batch=B, alpha=alpha, nseg=B if last else nseg)
    return h
```

```python
import functools

import jax
import jax.numpy as jnp
from jax import lax
from jax.experimental import pallas as pl
from jax.experimental.pallas import tpu as pltpu

F32 = jnp.float32
BF16 = jnp.bfloat16

D_MODEL = 1024
D_FF = 2816
N_MOD = 9
POOL_WINDOWS = (2, 4, 8, 16)
POOL_GROUP = 128
POOL_WIDTH = POOL_GROUP * len(POOL_WINDOWS)
GLA_HEADS = 4
GLA_DK = 64
GLA_DV = 128
GLA_KW = GLA_HEADS * GLA_DK
GLA_WIDTH = GLA_HEADS * GLA_DV
GATE_RANK = 16
GATE_TAU = 16.0
GLA_CHUNK = 64
GRID_W = 64
LN_EPS = 1e-6
MAIN_COLS = POOL_WIDTH + 2 * GLA_KW + 2 * GLA_WIDTH
P_COLS = POOL_WIDTH + 2 * GLA_WIDTH
QK_COLS = 3 * GLA_KW
GD_PAD = 256

VMEM_LIMIT_BYTES = 56 * 1024 * 1024
FFN_TOKENS = 512
BACK_TOKENS = 1024
FFN_CHUNK = 256
FFN_ROWS = 256
SEQ_TOKENS = 256
GLA_TOKENS = 1024
SUB_TOKENS = 128
GLA_GROUP = 2
CUM_TOKENS = 256
POOL_TOKENS = 256
HALO = 16


def _dot(a, b):
    return jnp.dot(a, b, preferred_element_type=F32)


def _dot_nt(a, b):
    return lax.dot_general(a, b, (((1,), (1,)), ((), ())), preferred_element_type=F32)


def _dot_tn(a, b):
    return lax.dot_general(a, b, (((0,), (0,)), ((), ())), preferred_element_type=F32)


def _silu(x):
    return x / (1.0 + jnp.exp(-x))


def _layernorm(z):
    mu = jnp.mean(z, axis=-1, keepdims=True)
    zc = z - mu
    var = jnp.mean(zc * zc, axis=-1, keepdims=True)
    return zc * lax.rsqrt(var + LN_EPS)


def _params(*sem):
    return pltpu.CompilerParams(dimension_semantics=sem, vmem_limit_bytes=VMEM_LIMIT_BYTES)


def _resident(shape):
    nd = len(shape)
    return pl.BlockSpec(shape, lambda *_: (0,) * nd, pipeline_mode=pl.Buffered(1))


def _ada_kernel(c_ref, w_ref, b_ref, o_ref):
    s = _silu(c_ref[...])
    o_ref[0] = _dot(s.astype(BF16), w_ref[0].astype(BF16)) + b_ref[0]


def _ada(cond, w_ada, b_ada):
    depth, d, n = w_ada.shape
    rows = cond.shape[0]
    tn = 1024
    return pl.pallas_call(
        _ada_kernel,
        grid=(depth, n // tn),
        in_specs=[
            pl.BlockSpec((rows, d), lambda l, j: (0, 0)),
            pl.BlockSpec((1, d, tn), lambda l, j: (l, 0, j)),
            pl.BlockSpec((1, 1, tn), lambda l, j: (l, 0, j)),
        ],
        out_specs=pl.BlockSpec((1, rows, tn), lambda l, j: (l, 0, j)),
        out_shape=jax.ShapeDtypeStruct((depth, rows, n), F32),
        compiler_params=_params("arbitrary", "arbitrary"),
        name="ada",
    )(cond, w_ada, b_ada.reshape(depth, 1, n))


def _ffn_rows(xr, mod_ref, k0, w_in_ref, w_out_ref, a_ref, rs, g_ref, b_ref, alpha):
    shift = mod_ref[0, k0:k0 + 1, :]
    scale = mod_ref[0, k0 + 1:k0 + 2, :]
    gate = mod_ref[0, k0 + 2:k0 + 3, :]
    xm = (xr * (1.0 + scale) + shift).astype(BF16)
    for j in range(D_FF // FFN_CHUNK):
        lo = j * FFN_CHUNK
        g = _dot(xm, w_in_ref[0, :, lo:lo + FFN_CHUNK])
        u = _dot(xm, w_in_ref[0, :, D_FF + lo:D_FF + lo + FFN_CHUNK])
        a_ref[rs, lo:lo + FFN_CHUNK] = (_silu(g) * u).astype(BF16)
    y = _dot(a_ref[rs, :], w_out_ref[0])
    z = alpha * xr + (0.5 * gate) * y
    return _layernorm(z) * g_ref[0] + b_ref[0]


def _inproj_groups(h_ref, groups, mod_ref, w_ref, gup_ref, gb_ref, trif_ref, trib_ref, p_ref, qk_ref, tot_ref):
    shift = mod_ref[0, 3:4, :]
    scale = mod_ref[0, 4:5, :]
    xms = [(h_ref[0, rs, :] * (1.0 + scale) + shift).astype(BF16) for rs in groups]
    gds = [_dot(xm, w_ref[0, :, MAIN_COLS:MAIN_COLS + GD_PAD]) for xm in xms]
    zs = [_dot(gd.astype(BF16), gup_ref[0]) + gb_ref[0] for gd in gds]

    def wide(r):
        rs, xm = groups[r], xms[r]
        qk = _dot(xm, w_ref[0, :, POOL_WIDTH:POOL_WIDTH + 2 * GLA_KW])
        p_ref[0, rs, 0:POOL_WIDTH] = _dot(xm, w_ref[0, :, 0:POOL_WIDTH]).astype(BF16)
        for j in (1, 2):
            p_ref[0, rs, j * 512:(j + 1) * 512] = _dot(xm, w_ref[0, :, 512 + j * 512:1024 + j * 512]).astype(BF16)
        return qk

    qks = [wide(0)]
    cums = []
    for z in zs:
        la = (jnp.minimum(z, 0.0) - jnp.log1p(jnp.exp(-jnp.abs(z)))) * (1.0 / GATE_TAU)
        la_b = la.astype(BF16)
        cums.append([_dot(tri_ref[...], la_b[:, d * GLA_KW:(d + 1) * GLA_KW])
                     for d, tri_ref in enumerate((trif_ref, trib_ref))])
    qks += [wide(r) for r in range(1, len(groups))]
    for r, (rs, qk) in enumerate(zip(groups, qks)):
        q = qk[:, :GLA_KW] * (GLA_DK ** -0.5)
        k = qk[:, GLA_KW:]
        for d, bcs in enumerate(cums[r]):
            cs = slice(d * GLA_KW, (d + 1) * GLA_KW)
            tots = []
            for c in range(CUM_TOKENS // GLA_CHUNK):
                e = c * GLA_CHUNK + (GLA_CHUNK - 1 if d == 0 else 0)
                t = bcs[e:e + 1, :]
                row = r * (CUM_TOKENS // GLA_CHUNK) + c
                tot_ref[0, row:row + 1, cs] = t
                tots.append(jnp.broadcast_to(t, (GLA_CHUNK, GLA_KW)))
            tot = jnp.concatenate(tots, axis=0)
            base = d * QK_COLS
            qk_ref[0, rs, base:base + GLA_KW] = (q * jnp.exp(bcs)).astype(BF16)
            qk_ref[0, rs, base + GLA_KW:base + 2 * GLA_KW] = (k * jnp.exp(-bcs)).astype(BF16)
            qk_ref[0, rs, base + 2 * GLA_KW:base + 3 * GLA_KW] = (k * jnp.exp(tot - bcs)).astype(BF16)


def _front_kernel(*refs, n_batch, alpha, first_layer):
    if first_layer:
        x_ref, pos_ref, ctx_ref = refs[:3]
        refs = refs[3:]
    else:
        h_ref = refs[0]
        refs = refs[1:]
    (mod_ref, w_in_ref, w_out_ref, g_ref, b_ref, wmix_ref, gup_ref, gb_ref, trif_ref, trib_ref,
     o_ref, p_ref, qk_ref, tot_ref, a_ref) = refs
    is_ctx = pl.program_id(0) == n_batch
    groups = [slice(r * FFN_ROWS, (r + 1) * FFN_ROWS) for r in range(o_ref.shape[1] // FFN_ROWS)]
    for rs in groups:
        if first_layer:
            xr = _layernorm(jnp.where(is_ctx, ctx_ref[0, rs, :], x_ref[0, rs, :] + pos_ref[rs, :]))
        else:
            xr = h_ref[0, rs, :]
        o_ref[0, rs, :] = _ffn_rows(xr, mod_ref, 0, w_in_ref, w_out_ref, a_ref, rs, g_ref, b_ref, alpha)
    _inproj_groups(o_ref, groups, mod_ref, wmix_ref, gup_ref, gb_ref, trif_ref, trib_ref, p_ref, qk_ref, tot_ref)


def _layer_block(arr, l):
    nd = arr.ndim
    return pl.BlockSpec((1,) + arr.shape[1:], lambda *_: (l,) + (0,) * (nd - 1), pipeline_mode=pl.Buffered(1))


def _front(src, mod, l, w_in, w_out, ln_g, ln_b, w_mix, gup, gbias, tri_f, tri_b, *, n_batch, alpha):
    first_layer = isinstance(src, tuple)
    B = n_batch
    nseg = B + 1
    tm = FFN_TOKENS
    if first_layer:
        x, pos, ctx_flat = src
        _, L, D = x.shape
        nt = L // tm
        src_specs = [
            pl.BlockSpec((1, tm, D), lambda b, i: (jnp.minimum(b, B - 1), jnp.where(b == B, nt - 1, i), 0)),
            pl.BlockSpec((tm, D), lambda b, i: (jnp.where(b == B, nt - 1, i), 0)),
            pl.BlockSpec((1, tm, D), lambda b, i: (0, jnp.where(b == B, i, 0), 0)),
        ]
    else:
        src = (src,)
        _, L, D = src[0].shape
        nt = L // tm
        src_specs = [pl.BlockSpec((1, tm, D), lambda b, i: (b, i, 0))]
    return pl.pallas_call(
        functools.partial(_front_kernel, n_batch=B, alpha=alpha, first_layer=first_layer),
        grid=(nseg, nt),
        in_specs=src_specs + [
            pl.BlockSpec((1, N_MOD, D), lambda b, i: (b, 0, 0)),
            _layer_block(w_in, l), _layer_block(w_out, l), _layer_block(ln_g, 3 * l), _layer_block(ln_b, 3 * l),
            _layer_block(w_mix, l), _layer_block(gup, l), _layer_block(gbias, l),
            _resident(tri_f.shape), _resident(tri_b.shape),
        ],
        out_specs=[
            pl.BlockSpec((1, tm, D), lambda b, i: (b, i, 0)),
            pl.BlockSpec((1, tm, P_COLS), lambda b, i: (b, i, 0)),
            pl.BlockSpec((1, tm, 2 * QK_COLS), lambda b, i: (b, i, 0)),
            pl.BlockSpec((1, tm // GLA_CHUNK, 2 * GLA_KW), lambda b, i: (b, i, 0)),
        ],
        out_shape=[
            jax.ShapeDtypeStruct((nseg, L, D), F32),
            jax.ShapeDtypeStruct((nseg, L, P_COLS), BF16),
            jax.ShapeDtypeStruct((nseg, L, 2 * QK_COLS), BF16),
            jax.ShapeDtypeStruct((nseg, L // GLA_CHUNK, 2 * GLA_KW), F32),
        ],
        scratch_shapes=[pltpu.VMEM((tm, D_FF), BF16)],
        compiler_params=_params("arbitrary", "arbitrary"),
        name="front",
    )(*src, mod, w_in, w_out, ln_g, ln_b, w_mix, gup, gbias, tri_f, tri_b)


def _gla_block(qk_ref, v_ref, tot_row, st_ref, *, fwd, tokens):
    pair = 2 * GLA_DK
    d = 0 if fwd else 1
    lane = lax.broadcasted_iota(jnp.int32, (pair, pair), 1)
    first_head = lane < GLA_DK
    head_mask = [jnp.where(first_head, 1.0, 0.0).astype(BF16), jnp.where(first_head, 0.0, 1.0).astype(BF16)]
    srow = lax.broadcasted_iota(jnp.int32, (2 * SUB_TOKENS, SUB_TOKENS), 0) % SUB_TOKENS
    scol = lax.broadcasted_iota(jnp.int32, (2 * SUB_TOKENS, SUB_TOKENS), 1)
    keep = ((srow // GLA_CHUNK) == (scol // GLA_CHUNK)) & ((scol <= srow) if fwd else (scol >= srow))

    nchunk = tokens // GLA_CHUNK
    npair = GLA_HEADS // 2
    chunk_rows = [slice(c * GLA_CHUNK, (c + 1) * GLA_CHUNK) for c in range(nchunk)]
    sub_rows = [slice(sb * SUB_TOKENS, (sb + 1) * SUB_TOKENS) for sb in range(tokens // SUB_TOKENS)]

    upd = {(c, p): _dot_tn(v_ref[0, rs, 2 * p * GLA_DV:(2 * p + 2) * GLA_DV],
                           qk_ref[0, rs, 2 * GLA_KW + p * pair:2 * GLA_KW + (p + 1) * pair])
           for c, rs in enumerate(chunk_rows) for p in range(npair)}
    st = [st_ref[p * pair:(p + 1) * pair, :] for p in range(npair)]
    st_start = {}
    for c in (range(nchunk) if fwd else range(nchunk - 1, -1, -1)):
        for p in range(npair):
            st_start[c, p] = st[p].astype(BF16)
            decay = jnp.exp(tot_row(c, slice(d * GLA_KW + p * pair, d * GLA_KW + (p + 1) * pair)))
            u = upd[c, p]
            st[p] = st[p] * decay + jnp.where(first_head, u[:GLA_DV], u[GLA_DV:])
    for p in range(npair):
        st_ref[p * pair:(p + 1) * pair, :] = st[p]

    zeros_half = jnp.zeros((GLA_CHUNK, pair), BF16)
    group_size = len(sub_rows) if fwd else GLA_GROUP
    for g0 in range(0, len(sub_rows), group_size):
        group = list(range(g0, min(g0 + group_size, len(sub_rows))))
        q_h = {}
        scores = {}
        for sb in group:
            rs = sub_rows[sb]
            for p in range(npair):
                qp = qk_ref[0, rs, p * pair:(p + 1) * pair]
                q_h[sb, p] = [qp * head_mask[0], qp * head_mask[1]]
                k_in = qk_ref[0, rs, GLA_KW + p * pair:GLA_KW + (p + 1) * pair]
                scores[sb, p] = _dot_nt(jnp.concatenate(q_h[sb, p], axis=0), k_in)
        inter = {}
        for sb in group:
            for p in range(npair):
                c0 = sb * (SUB_TOKENS // GLA_CHUNK)
                st_cat = jnp.concatenate([st_start[c0, p], st_start[c0 + 1, p]], axis=1)
                q0, q1 = q_h[sb, p]
                q_inter = jnp.concatenate(
                    [jnp.concatenate([q0[:GLA_CHUNK], zeros_half, q1[:GLA_CHUNK], zeros_half], axis=0),
                     jnp.concatenate([zeros_half, q0[GLA_CHUNK:], zeros_half, q1[GLA_CHUNK:]], axis=0)], axis=1)
                inter[sb, p] = _dot_nt(q_inter, st_cat)
        for sb in group:
            rs = sub_rows[sb]
            o_heads = []
            for p in range(npair):
                sc = jnp.where(keep, scores[sb, p], 0.0).astype(BF16)
                for hh in range(2):
                    h = 2 * p + hh
                    hs = slice(hh * SUB_TOKENS, (hh + 1) * SUB_TOKENS)
                    o_heads.append(_dot(sc[hs], v_ref[0, rs, h * GLA_DV:(h + 1) * GLA_DV]) + inter[sb, p][hs])
            yield sb, jnp.concatenate(o_heads, axis=1)


def _gla_kernel(*refs, tokens, nblk, is_ctx):
    if is_ctx:
        qkf_ref, qkb_ref, v_ref, r_ref, tot_ref, ng_ref, o_ref, sfin_ref, of_ref, st_ref = refs
    else:
        qkf_ref, qkb_ref, v_ref, r_ref, tot_ref, s0_ref, ng_ref, o_ref, of_ref, st_ref = refs
    ph = pl.program_id(1)
    j = pl.program_id(2)
    jb = jnp.where(ph == 0, j, nblk - 1 - j)
    base = pl.multiple_of(jb * tokens, tokens)
    nchunk = tokens // GLA_CHUNK

    @pl.when(j == 0)
    def _():
        st_ref[...] = jnp.zeros_like(st_ref) if is_ctx else s0_ref[0, 0]

    if is_ctx:
        odd = (pl.program_id(0) % 2) == 1

        def tot_row(c, ls):
            return jnp.where(odd, tot_ref[0, nchunk + c:nchunk + c + 1, ls], tot_ref[0, c:c + 1, ls])
    else:
        def tot_row(c, ls):
            return tot_ref[0, c:c + 1, ls]

    @pl.when(ph == 0)
    def _():
        for sb, o in _gla_block(qkf_ref, v_ref, tot_row, st_ref, fwd=True, tokens=tokens):
            of_ref[pl.ds(base + sb * SUB_TOKENS, SUB_TOKENS), :] = o

    @pl.when(ph == 1)
    def _():
        for sb, o in _gla_block(qkb_ref, v_ref, tot_row, st_ref, fwd=False, tokens=tokens):
            rs = slice(sb * SUB_TOKENS, (sb + 1) * SUB_TOKENS)
            o = o + of_ref[pl.ds(base + sb * SUB_TOKENS, SUB_TOKENS), :]
            normed = []
            for h in range(GLA_HEADS):
                oh = o[:, h * GLA_DV:(h + 1) * GLA_DV]
                normed.append(oh * lax.rsqrt(jnp.mean(oh * oh, axis=-1, keepdims=True) + LN_EPS))
            on = jnp.concatenate(normed, axis=1) * ng_ref[...]
            o_ref[0, rs, :] = (on * _silu(r_ref[0, rs, :].astype(F32))).astype(BF16)

    if is_ctx:
        @pl.when(j == nblk - 1)
        def _():
            sfin_ref[0, 0] = st_ref[...]


def _gla_ctx(p, qk, tot, norm_g, *, n_batch):
    _, L, _ = p.shape
    tb = SEQ_TOKENS
    st_shape = ((GLA_HEADS // 2) * GLA_DV, 2 * GLA_DK)
    return pl.pallas_call(
        functools.partial(_gla_kernel, tokens=tb, nblk=1, is_ctx=True),
        grid=(n_batch, 2, 1),
        in_specs=[
            pl.BlockSpec((1, tb, QK_COLS), lambda b, ph, j: (n_batch, b, 0)),
            pl.BlockSpec((1, tb, QK_COLS), lambda b, ph, j: (n_batch, b, 1)),
            pl.BlockSpec((1, tb, GLA_WIDTH), lambda b, ph, j: (n_batch, b, 1)),
            pl.BlockSpec((1, tb, GLA_WIDTH), lambda b, ph, j: (n_batch, b, 2)),
            pl.BlockSpec((1, 2 * tb // GLA_CHUNK, 2 * GLA_KW), lambda b, ph, j: (n_batch, b // 2, 0)),
            _resident((1, GLA_WIDTH)),
        ],
        out_specs=[
            pl.BlockSpec((1, tb, GLA_WIDTH), lambda b, ph, j: (0, b, 0)),
            pl.BlockSpec((1, 1) + st_shape, lambda b, ph, j: (b, ph, 0, 0)),
        ],
        out_shape=[
            jax.ShapeDtypeStruct((1, L, GLA_WIDTH), BF16),
            jax.ShapeDtypeStruct((n_batch, 2) + st_shape, F32),
        ],
        scratch_shapes=[
            pltpu.VMEM((tb, GLA_WIDTH), F32),
            pltpu.VMEM(st_shape, F32),
        ],
        compiler_params=_params("arbitrary", "arbitrary", "arbitrary"),
        name="gla_ctx",
    )(qk, qk, p, p, tot, norm_g.reshape(1, GLA_WIDTH))


def _gla_latent(p, qk, tot, s0, norm_g, *, n_batch):
    _, L, _ = p.shape
    tb = GLA_TOKENS
    nblk = L // tb
    st_shape = ((GLA_HEADS // 2) * GLA_DV, 2 * GLA_DK)

    def visited(ph, j):
        return jnp.where(ph == 0, j, nblk - 1 - j)

    def fwd_only(cidx):
        return lambda b, ph, j: (b, jnp.where(ph == 0, j, nblk - 1), cidx)

    def bwd_only(cidx):
        return lambda b, ph, j: (b, jnp.where(ph == 0, nblk - 1, nblk - 1 - j), cidx)

    return pl.pallas_call(
        functools.partial(_gla_kernel, tokens=tb, nblk=nblk, is_ctx=False),
        grid=(n_batch, 2, nblk),
        in_specs=[
            pl.BlockSpec((1, tb, QK_COLS), fwd_only(0)),
            pl.BlockSpec((1, tb, QK_COLS), bwd_only(1)),
            pl.BlockSpec((1, tb, GLA_WIDTH), lambda b, ph, j: (b, visited(ph, j), 1)),
            pl.BlockSpec((1, tb, GLA_WIDTH), bwd_only(2)),
            pl.BlockSpec((1, tb // GLA_CHUNK, 2 * GLA_KW), lambda b, ph, j: (b, visited(ph, j), 0)),
            pl.BlockSpec((1, 1) + st_shape, lambda b, ph, j: (b, ph, 0, 0)),
            _resident((1, GLA_WIDTH)),
        ],
        out_specs=pl.BlockSpec((1, tb, GLA_WIDTH), bwd_only(0)),
        out_shape=jax.ShapeDtypeStruct((n_batch, L, GLA_WIDTH), BF16),
        scratch_shapes=[
            pltpu.VMEM((L, GLA_WIDTH), F32),
            pltpu.VMEM(st_shape, F32),
        ],
        compiler_params=_params("arbitrary", "arbitrary", "arbitrary"),
        name="gla_latent",
    )(qk, qk, p, p, tot, s0, norm_g.reshape(1, GLA_WIDTH))


def _back_kernel(h_ref, u_ref, up_ref, un_ref, glat_ref, gctx_ref, mod_ref, band_ref, pw_ref, ps_ref, wmix_ref,
                 g1_ref, b1_ref, w_in_ref, w_out_ref, g2_ref, b2_ref, o_ref, ue_ref, a_ref, *, n_batch, alpha):
    tm = h_ref.shape[1]
    i = pl.program_id(1)
    is_ctx = pl.program_id(0) == n_batch
    first = is_ctx | (i == 0)
    last = is_ctx | (i == pl.num_programs(1) - 1)
    no_halo = jnp.zeros((HALO, POOL_WIDTH), BF16)
    ue_ref[0:HALO, :] = jnp.where(first, no_halo, up_ref[0])
    ue_ref[HALO:HALO + tm, :] = u_ref[0]
    ue_ref[HALO + tm:, :] = jnp.where(last, no_halo, un_ref[0])

    pos = lax.broadcasted_iota(jnp.int32, (POOL_TOKENS, 1), 0)
    nsub = tm // POOL_TOKENS

    groups = [slice(s * POOL_TOKENS, (s + 1) * POOL_TOKENS) for s in range(nsub)]
    wsums = [[_dot(band_ref[0, g], ue_ref[rs.start:rs.stop + 2 * HALO, g * POOL_GROUP:(g + 1) * POOL_GROUP])
              for g in range(len(POOL_WINDOWS))] for rs in groups]
    pool_ys = []
    for s, rs in enumerate(groups):
        starts = first if s == 0 else is_ctx
        ends = last if s == nsub - 1 else is_ctx
        room_lo = jnp.where(starts, pos, POOL_TOKENS)
        room_hi = jnp.where(ends, POOL_TOKENS - 1 - pos, POOL_TOKENS)
        yg = []
        for g, w in enumerate(POOL_WINDOWS):
            cs = slice(g * POOL_GROUP, (g + 1) * POOL_GROUP)
            lo, hi = w // 2, w - 1 - w // 2
            cnt = (jnp.minimum(room_lo, lo) + jnp.minimum(room_hi, hi) + 1).astype(F32)
            pooled = wsums[s][g] / cnt - ue_ref[HALO + rs.start:HALO + rs.stop, cs].astype(F32)
            yg.append(_dot(pooled.astype(BF16), pw_ref[0, g]))
        pool_ys.append((jnp.concatenate(yg, axis=1) * ps_ref[0]).astype(BF16))
    ys = []
    for rs, pool_y in zip(groups, pool_ys):
        gla = jnp.where(is_ctx, gctx_ref[0, rs, :], glat_ref[0, rs, :])
        ys.append(_dot(pool_y, wmix_ref[0, :POOL_WIDTH, :]) + _dot(gla, wmix_ref[0, POOL_WIDTH:, :]))
    for rs, y in zip(groups, ys):
        z = alpha * h_ref[0, rs, :] + mod_ref[0, 5:6, :] * y
        o_ref[0, rs, :] = _layernorm(z) * g1_ref[0] + b1_ref[0]
    for rs in groups:
        o_ref[0, rs, :] = _ffn_rows(o_ref[0, rs, :], mod_ref, 6, w_in_ref, w_out_ref, a_ref, rs, g2_ref, b2_ref,
                                    alpha)


def _pool_bands():
    t = jnp.arange(POOL_TOKENS)[:, None]
    j = jnp.arange(POOL_TOKENS + 2 * HALO)[None, :] - HALO
    inside = (j >= 0) & (j < POOL_TOKENS)
    bands = []
    for w in POOL_WINDOWS:
        lo, hi = w // 2, w - 1 - w // 2
        bands.append((j >= t - lo) & (j <= t + hi))
    bands = jnp.stack(bands)
    return jnp.stack([bands, bands & inside]).astype(BF16)


def _back(h, p, g_lat, g_ctx, mod, l, bands, pool_w, pool_scale, w_mix_out, ln_g, ln_b, w_in, w_out,
          *, n_batch, alpha, nseg):
    _, L, D = h.shape
    tm = BACK_TOKENS
    nt = L // tm
    hb = tm // HALO
    return pl.pallas_call(
        functools.partial(_back_kernel, n_batch=n_batch, alpha=alpha),
        grid=(nseg, nt),
        in_specs=[
            pl.BlockSpec((1, tm, D), lambda b, i: (b, i, 0)),
            pl.BlockSpec((1, tm, POOL_WIDTH), lambda b, i: (b, i, 0)),
            pl.BlockSpec((1, HALO, POOL_WIDTH), lambda b, i: (b, jnp.maximum(i * hb - 1, 0), 0)),
            pl.BlockSpec((1, HALO, POOL_WIDTH), lambda b, i: (b, jnp.minimum((i + 1) * hb, nt * hb - 1), 0)),
            pl.BlockSpec((1, tm, GLA_WIDTH),
                         lambda b, i: (jnp.minimum(b, n_batch - 1), jnp.where(b == n_batch, nt - 1, i), 0)),
            pl.BlockSpec((1, tm, GLA_WIDTH), lambda b, i: (0, jnp.where(b == n_batch, i, 0), 0)),
            pl.BlockSpec((1, N_MOD, D), lambda b, i: (b, 0, 0)),
            pl.BlockSpec((1,) + bands.shape[1:], lambda b, i: (jnp.where(b == n_batch, 1, 0), 0, 0, 0)),
            _layer_block(pool_w, l), _layer_block(pool_scale, l), _layer_block(w_mix_out, l),
            _layer_block(ln_g, 3 * l + 1), _layer_block(ln_b, 3 * l + 1),
            _layer_block(w_in, l), _layer_block(w_out, l),
            _layer_block(ln_g, 3 * l + 2), _layer_block(ln_b, 3 * l + 2),
        ],
        out_specs=pl.BlockSpec((1, tm, D), lambda b, i: (b, i, 0)),
        out_shape=jax.ShapeDtypeStruct((nseg, L, D), F32),
        scratch_shapes=[pltpu.VMEM((tm + 2 * HALO, POOL_WIDTH), BF16), pltpu.VMEM((tm, D_FF), BF16)],
        compiler_params=_params("arbitrary", "arbitrary"),
        name="back",
    )(h, p, p, p, g_lat, g_ctx, mod, bands, pool_w, pool_scale, w_mix_out, ln_g, ln_b, w_in, w_out, ln_g, ln_b)


def _pos_embed_2d(L):
    rows = L // GRID_W
    quarter = D_MODEL // 4
    omega = 1.0 / (10000.0 ** (jnp.arange(quarter, dtype=F32) / quarter))

    def enc(n):
        a = jnp.arange(n, dtype=F32)[:, None] * omega
        return jnp.concatenate([jnp.sin(a), jnp.cos(a)], axis=-1)

    return jnp.concatenate([jnp.repeat(enc(rows), GRID_W, axis=0), jnp.tile(enc(GRID_W), (rows, 1))], axis=-1)


def kernel(x, c, ctx, c_ctx, w_ada, b_ada, ln_g, ln_b, ffa_w_in, ffa_w_out, mix_w_in, pool_w, pool_scale,
           gate_up_f, gate_bias_f, gate_up_b, gate_bias_b, gla_norm_g, mix_w_out, ffb_w_in, ffb_w_out):
    B, L, D = x.shape
    LC = ctx.shape[1]
    depth = w_ada.shape[0]
    assert D == D_MODEL and LC == SEQ_TOKENS == POOL_TOKENS == FFN_ROWS == CUM_TOKENS and B * LC == L
    assert L % FFN_TOKENS == 0 and L % BACK_TOKENS == 0 and L % GLA_TOKENS == 0
    alpha = (2.0 * depth) ** 0.25
    nseg = B + 1

    rows = -(-nseg // 8) * 8
    cond = jnp.concatenate([c, c_ctx[None, :], jnp.zeros((rows - nseg, D), F32)], axis=0)
    mod = _ada(cond, w_ada, b_ada)[:, :nseg].reshape(depth, nseg, N_MOD, D)

    zeros_gd = jnp.zeros((depth, D, GD_PAD - 2 * GATE_RANK), BF16)
    w_mix_in = jnp.concatenate([mix_w_in.astype(BF16), zeros_gd], axis=-1)
    gup = jnp.zeros((depth, GD_PAD, 2 * GLA_KW), F32)
    gup = gup.at[:, :GATE_RANK, :GLA_KW].set(gate_up_f)
    gup = gup.at[:, GATE_RANK:2 * GATE_RANK, GLA_KW:].set(gate_up_b).astype(BF16)
    gbias = jnp.concatenate([gate_bias_f, gate_bias_b], axis=-1).reshape(depth, 1, 2 * GLA_KW)
    ffa_in, ffa_out = ffa_w_in.astype(BF16), ffa_w_out.astype(BF16)
    ffb_in, ffb_out = ffb_w_in.astype(BF16), ffb_w_out.astype(BF16)
    w_mix_out = mix_w_out.astype(BF16)
    pw = pool_w.astype(BF16)
    ps = pool_scale.reshape(depth, 1, POOL_WIDTH)
    lng = ln_g.reshape(depth * 3, 1, D)
    lnb = ln_b.reshape(depth * 3, 1, D)
    ti = jnp.arange(CUM_TOKENS)
    same_chunk = (ti[:, None] // GLA_CHUNK) == (ti[None, :] // GLA_CHUNK)
    tri_f = (same_chunk & (ti[None, :] <= ti[:, None])).astype(BF16)
    tri_b = (same_chunk & (ti[None, :] >= ti[:, None])).astype(BF16)
    bands = _pool_bands()

    h = (x, _pos_embed_2d(L), ctx.reshape(1, L, D))
    for l in range(depth):
        last = l == depth - 1
        h, p, qk, tot = _front(h, mod[l], l, ffa_in, ffa_out, lng, lnb, w_mix_in, gup, gbias, tri_f, tri_b,
                               n_batch=B, alpha=alpha)
        g_ctx, s_ctx = _gla_ctx(p, qk, tot, gla_norm_g[l], n_batch=B)
        g_lat = _gla_latent(p, qk, tot, s_ctx, gla_norm_g[l], n_batch=B)
        h = _back(h, p, g_lat, g_ctx, mod[l], l, bands, pw, ps, w_mix_out, lng, lnb, ffb_in, ffb_out,
                  n_batch=B, alpha=alpha, nseg=B if last else nseg)
    return h
```

```python
import functools

import jax
import jax.numpy as jnp
from jax import lax
from jax.experimental import pallas as pl
from jax.experimental.pallas import tpu as pltpu

F32 = jnp.float32
BF16 = jnp.bfloat16

D_MODEL = 1024
D_FF = 2816
N_MOD = 9
POOL_WINDOWS = (2, 4, 8, 16)
POOL_GROUP = 128
POOL_WIDTH = POOL_GROUP * len(POOL_WINDOWS)
GLA_HEADS = 4
GLA_DK = 64
GLA_DV = 128
GLA_KW = GLA_HEADS * GLA_DK
GLA_WIDTH = GLA_HEADS * GLA_DV
GATE_RANK = 16
GATE_TAU = 16.0
GLA_CHUNK = 64
GRID_W = 64
LN_EPS = 1e-6
MAIN_COLS = POOL_WIDTH + 2 * GLA_KW + 2 * GLA_WIDTH
P_COLS = POOL_WIDTH + 2 * GLA_WIDTH
QK_COLS = 3 * GLA_KW
GD_PAD = 128

VMEM_LIMIT_BYTES = 56 * 1024 * 1024
FFN_TOKENS = 512
BACK_TOKENS = 1024
FFN_CHUNK = 256
FFN_ROWS = 256
SEQ_TOKENS = 256
GLA_TOKENS = 1024
SUB_TOKENS = 128
GLA_GROUP = 2
CUM_TOKENS = 256
POOL_TOKENS = 256
HALO = 16


def _dot(a, b):
    return jnp.dot(a, b, preferred_element_type=F32)


def _dot_nt(a, b):
    return lax.dot_general(a, b, (((1,), (1,)), ((), ())), preferred_element_type=F32)


def _dot_tn(a, b):
    return lax.dot_general(a, b, (((0,), (0,)), ((), ())), preferred_element_type=F32)


def _silu(x):
    return x / (1.0 + jnp.exp(-x))


def _layernorm(z):
    mu = jnp.mean(z, axis=-1, keepdims=True)
    zc = z - mu
    var = jnp.mean(zc * zc, axis=-1, keepdims=True)
    return zc * lax.rsqrt(var + LN_EPS)


def _params(*sem):
    return pltpu.CompilerParams(dimension_semantics=sem, vmem_limit_bytes=VMEM_LIMIT_BYTES)


def _resident(shape):
    nd = len(shape)
    return pl.BlockSpec(shape, lambda *_: (0,) * nd, pipeline_mode=pl.Buffered(1))


def _ada_kernel(c_ref, w_ref, b_ref, o_ref):
    s = _silu(c_ref[...])
    o_ref[0] = _dot(s.astype(BF16), w_ref[0].astype(BF16)) + b_ref[0]


def _ada(cond, w_ada, b_ada):
    depth, d, n = w_ada.shape
    rows = cond.shape[0]
    tn = 1024
    return pl.pallas_call(
        _ada_kernel,
        grid=(depth, n // tn),
        in_specs=[
            pl.BlockSpec((rows, d), lambda l, j: (0, 0)),
            pl.BlockSpec((1, d, tn), lambda l, j: (l, 0, j)),
            pl.BlockSpec((1, 1, tn), lambda l, j: (l, 0, j)),
        ],
        out_specs=pl.BlockSpec((1, rows, tn), lambda l, j: (l, 0, j)),
        out_shape=jax.ShapeDtypeStruct((depth, rows, n), F32),
        compiler_params=_params("arbitrary", "arbitrary"),
        name="ada",
    )(cond, w_ada, b_ada.reshape(depth, 1, n))


def _ffn_rows(xr, mod_ref, k0, w_in_ref, w_out_ref, a_ref, rs, g_ref, b_ref, alpha):
    shift = mod_ref[0, k0:k0 + 1, :]
    scale = mod_ref[0, k0 + 1:k0 + 2, :]
    gate = mod_ref[0, k0 + 2:k0 + 3, :]
    xm = (xr * (1.0 + scale) + shift).astype(BF16)
    for j in range(D_FF // FFN_CHUNK):
        lo = j * FFN_CHUNK
        g = _dot(xm, w_in_ref[0, :, lo:lo + FFN_CHUNK])
        u = _dot(xm, w_in_ref[0, :, D_FF + lo:D_FF + lo + FFN_CHUNK])
        a_ref[rs, lo:lo + FFN_CHUNK] = (_silu(g) * u).astype(BF16)
    y = _dot(a_ref[rs, :], w_out_ref[0])
    z = alpha * xr + (0.5 * gate) * y
    return _layernorm(z) * g_ref[0] + b_ref[0]


def _inproj_groups(h_ref, groups, mod_ref, w_ref, gup_ref, gb_ref, trif_ref, trib_ref, p_ref, qk_ref, tot_ref):
    shift = mod_ref[0, 3:4, :]
    scale = mod_ref[0, 4:5, :]
    xms = [(h_ref[0, rs, :] * (1.0 + scale) + shift).astype(BF16) for rs in groups]
    gds = [_dot(xm, w_ref[0, :, MAIN_COLS:MAIN_COLS + GD_PAD]) for xm in xms]
    zs = [_dot(gd.astype(BF16), gup_ref[0]) + gb_ref[0] for gd in gds]

    def wide(r):
        rs, xm = groups[r], xms[r]
        qk = _dot(xm, w_ref[0, :, POOL_WIDTH:POOL_WIDTH + 2 * GLA_KW])
        p_ref[0, rs, 0:POOL_WIDTH] = _dot(xm, w_ref[0, :, 0:POOL_WIDTH]).astype(BF16)
        for j in (1, 2):
            p_ref[0, rs, j * 512:(j + 1) * 512] = _dot(xm, w_ref[0, :, 512 + j * 512:1024 + j * 512]).astype(BF16)
        return qk

    qks = [wide(0)]
    cums = []
    for z in zs:
        la = (jnp.minimum(z, 0.0) - jnp.log1p(jnp.exp(-jnp.abs(z)))) * (1.0 / GATE_TAU)
        la_b = la.astype(BF16)
        cums.append([_dot(tri_ref[...], la_b[:, d * GLA_KW:(d + 1) * GLA_KW])
                     for d, tri_ref in enumerate((trif_ref, trib_ref))])
    qks += [wide(r) for r in range(1, len(groups))]
    for r, (rs, qk) in enumerate(zip(groups, qks)):
        q = qk[:, :GLA_KW] * (GLA_DK ** -0.5)
        k = qk[:, GLA_KW:]
        for d, bcs in enumerate(cums[r]):
            cs = slice(d * GLA_KW, (d + 1) * GLA_KW)
            tots = []
            for c in range(CUM_TOKENS // GLA_CHUNK):
                e = c * GLA_CHUNK + (GLA_CHUNK - 1 if d == 0 else 0)
                t = bcs[e:e + 1, :]
                row = r * (CUM_TOKENS // GLA_CHUNK) + c
                tot_ref[0, row:row + 1, cs] = t
                tots.append(jnp.broadcast_to(t, (GLA_CHUNK, GLA_KW)))
            tot = jnp.concatenate(tots, axis=0)
            base = d * QK_COLS
            qk_ref[0, rs, base:base + GLA_KW] = (q * jnp.exp(bcs)).astype(BF16)
            qk_ref[0, rs, base + GLA_KW:base + 2 * GLA_KW] = (k * jnp.exp(-bcs)).astype(BF16)
            qk_ref[0, rs, base + 2 * GLA_KW:base + 3 * GLA_KW] = (k * jnp.exp(tot - bcs)).astype(BF16)


def _front_kernel(*refs, n_batch, alpha, first_layer):
    if first_layer:
        x_ref, pos_ref, ctx_ref = refs[:3]
        refs = refs[3:]
    else:
        h_ref = refs[0]
        refs = refs[1:]
    (mod_ref, w_in_ref, w_out_ref, g_ref, b_ref, wmix_ref, gup_ref, gb_ref, trif_ref, trib_ref,
     o_ref, p_ref, qk_ref, tot_ref, a_ref) = refs
    is_ctx = pl.program_id(0) == n_batch
    groups = [slice(r * FFN_ROWS, (r + 1) * FFN_ROWS) for r in range(o_ref.shape[1] // FFN_ROWS)]
    for rs in groups:
        if first_layer:
            xr = _layernorm(jnp.where(is_ctx, ctx_ref[0, rs, :], x_ref[0, rs, :] + pos_ref[rs, :]))
        else:
            xr = h_ref[0, rs, :]
        o_ref[0, rs, :] = _ffn_rows(xr, mod_ref, 0, w_in_ref, w_out_ref, a_ref, rs, g_ref, b_ref, alpha)
    _inproj_groups(o_ref, groups, mod_ref, wmix_ref, gup_ref, gb_ref, trif_ref, trib_ref, p_ref, qk_ref, tot_ref)


def _layer_block(arr, l):
    nd = arr.ndim
    return pl.BlockSpec((1,) + arr.shape[1:], lambda *_: (l,) + (0,) * (nd - 1), pipeline_mode=pl.Buffered(1))


def _front(src, mod, l, w_in, w_out, ln_g, ln_b, w_mix, gup, gbias, tri_f, tri_b, *, n_batch, alpha):
    first_layer = isinstance(src, tuple)
    B = n_batch
    nseg = B + 1
    tm = FFN_TOKENS
    if first_layer:
        x, pos, ctx_flat = src
        _, L, D = x.shape
        nt = L // tm
        src_specs = [
            pl.BlockSpec((1, tm, D), lambda b, i: (jnp.minimum(b, B - 1), jnp.where(b == B, nt - 1, i), 0)),
            pl.BlockSpec((tm, D), lambda b, i: (jnp.where(b == B, nt - 1, i), 0)),
            pl.BlockSpec((1, tm, D), lambda b, i: (0, jnp.where(b == B, i, 0), 0)),
        ]
    else:
        src = (src,)
        _, L, D = src[0].shape
        nt = L // tm
        src_specs = [pl.BlockSpec((1, tm, D), lambda b, i: (b, i, 0))]
    return pl.pallas_call(
        functools.partial(_front_kernel, n_batch=B, alpha=alpha, first_layer=first_layer),
        grid=(nseg, nt),
        in_specs=src_specs + [
            pl.BlockSpec((1, N_MOD, D), lambda b, i: (b, 0, 0)),
            _layer_block(w_in, l), _layer_block(w_out, l), _layer_block(ln_g, 3 * l), _layer_block(ln_b, 3 * l),
            _layer_block(w_mix, l), _layer_block(gup, l), _layer_block(gbias, l),
            _resident(tri_f.shape), _resident(tri_b.shape),
        ],
        out_specs=[
            pl.BlockSpec((1, tm, D), lambda b, i: (b, i, 0)),
            pl.BlockSpec((1, tm, P_COLS), lambda b, i: (b, i, 0)),
            pl.BlockSpec((1, tm, 2 * QK_COLS), lambda b, i: (b, i, 0)),
            pl.BlockSpec((1, tm // GLA_CHUNK, 2 * GLA_KW), lambda b, i: (b, i, 0)),
        ],
        out_shape=[
            jax.ShapeDtypeStruct((nseg, L, D), F32),
            jax.ShapeDtypeStruct((nseg, L, P_COLS), BF16),
            jax.ShapeDtypeStruct((nseg, L, 2 * QK_COLS), BF16),
            jax.ShapeDtypeStruct((nseg, L // GLA_CHUNK, 2 * GLA_KW), F32),
        ],
        scratch_shapes=[pltpu.VMEM((tm, D_FF), BF16)],
        compiler_params=_params("arbitrary", "arbitrary"),
        name="front",
    )(*src, mod, w_in, w_out, ln_g, ln_b, w_mix, gup, gbias, tri_f, tri_b)


def _gla_block(qk_ref, v_ref, tot_row, st_ref, *, fwd, tokens):
    pair = 2 * GLA_DK
    d = 0 if fwd else 1
    lane = lax.broadcasted_iota(jnp.int32, (pair, pair), 1)
    first_head = lane < GLA_DK
    head_mask = [jnp.where(first_head, 1.0, 0.0).astype(BF16), jnp.where(first_head, 0.0, 1.0).astype(BF16)]
    srow = lax.broadcasted_iota(jnp.int32, (2 * SUB_TOKENS, SUB_TOKENS), 0) % SUB_TOKENS
    scol = lax.broadcasted_iota(jnp.int32, (2 * SUB_TOKENS, SUB_TOKENS), 1)
    keep = ((srow // GLA_CHUNK) == (scol // GLA_CHUNK)) & ((scol <= srow) if fwd else (scol >= srow))

    nchunk = tokens // GLA_CHUNK
    npair = GLA_HEADS // 2
    chunk_rows = [slice(c * GLA_CHUNK, (c + 1) * GLA_CHUNK) for c in range(nchunk)]
    sub_rows = [slice(sb * SUB_TOKENS, (sb + 1) * SUB_TOKENS) for sb in range(tokens // SUB_TOKENS)]

    upd = {(c, p): _dot_tn(v_ref[0, rs, 2 * p * GLA_DV:(2 * p + 2) * GLA_DV],
                           qk_ref[0, rs, 2 * GLA_KW + p * pair:2 * GLA_KW + (p + 1) * pair])
           for c, rs in enumerate(chunk_rows) for p in range(npair)}
    st = [st_ref[p * pair:(p + 1) * pair, :] for p in range(npair)]
    st_start = {}
    for c in (range(nchunk) if fwd else range(nchunk - 1, -1, -1)):
        for p in range(npair):
            st_start[c, p] = st[p].astype(BF16)
            decay = jnp.exp(tot_row(c, slice(d * GLA_KW + p * pair, d * GLA_KW + (p + 1) * pair)))
            u = upd[c, p]
            st[p] = st[p] * decay + jnp.where(first_head, u[:GLA_DV], u[GLA_DV:])
    for p in range(npair):
        st_ref[p * pair:(p + 1) * pair, :] = st[p]

    zeros_half = jnp.zeros((GLA_CHUNK, pair), BF16)
    group_size = len(sub_rows) if fwd else GLA_GROUP
    for g0 in range(0, len(sub_rows), group_size):
        group = list(range(g0, min(g0 + group_size, len(sub_rows))))
        q_h = {}
        scores = {}
        for sb in group:
            rs = sub_rows[sb]
            for p in range(npair):
                qp = qk_ref[0, rs, p * pair:(p + 1) * pair]
                q_h[sb, p] = [qp * head_mask[0], qp * head_mask[1]]
                k_in = qk_ref[0, rs, GLA_KW + p * pair:GLA_KW + (p + 1) * pair]
                scores[sb, p] = _dot_nt(jnp.concatenate(q_h[sb, p], axis=0), k_in)
        inter = {}
        for sb in group:
            for p in range(npair):
                c0 = sb * (SUB_TOKENS // GLA_CHUNK)
                st_cat = jnp.concatenate([st_start[c0, p], st_start[c0 + 1, p]], axis=1)
                q0, q1 = q_h[sb, p]
                q_inter = jnp.concatenate(
                    [jnp.concatenate([q0[:GLA_CHUNK], zeros_half, q1[:GLA_CHUNK], zeros_half], axis=0),
                     jnp.concatenate([zeros_half, q0[GLA_CHUNK:], zeros_half, q1[GLA_CHUNK:]], axis=0)], axis=1)
                inter[sb, p] = _dot_nt(q_inter, st_cat)
        for sb in group:
            rs = sub_rows[sb]
            o_heads = []
            for p in range(npair):
                sc = jnp.where(keep, scores[sb, p], 0.0).astype(BF16)
                for hh in range(2):
                    h = 2 * p + hh
                    hs = slice(hh * SUB_TOKENS, (hh + 1) * SUB_TOKENS)
                    o_heads.append(_dot(sc[hs], v_ref[0, rs, h * GLA_DV:(h + 1) * GLA_DV]) + inter[sb, p][hs])
            yield sb, jnp.concatenate(o_heads, axis=1)


def _gla_kernel(*refs, tokens, nblk, is_ctx):
    if is_ctx:
        qkf_ref, qkb_ref, v_ref, r_ref, tot_ref, ng_ref, o_ref, sfin_ref, of_ref, st_ref = refs
    else:
        qkf_ref, qkb_ref, v_ref, r_ref, tot_ref, s0_ref, ng_ref, o_ref, of_ref, st_ref = refs
    ph = pl.program_id(1)
    j = pl.program_id(2)
    jb = jnp.where(ph == 0, j, nblk - 1 - j)
    base = pl.multiple_of(jb * tokens, tokens)
    nchunk = tokens // GLA_CHUNK

    @pl.when(j == 0)
    def _():
        st_ref[...] = jnp.zeros_like(st_ref) if is_ctx else s0_ref[0, 0]

    if is_ctx:
        odd = (pl.program_id(0) % 2) == 1

        def tot_row(c, ls):
            return jnp.where(odd, tot_ref[0, nchunk + c:nchunk + c + 1, ls], tot_ref[0, c:c + 1, ls])
    else:
        def tot_row(c, ls):
            return tot_ref[0, c:c + 1, ls]

    @pl.when(ph == 0)
    def _():
        for sb, o in _gla_block(qkf_ref, v_ref, tot_row, st_ref, fwd=True, tokens=tokens):
            of_ref[pl.ds(base + sb * SUB_TOKENS, SUB_TOKENS), :] = o

    @pl.when(ph == 1)
    def _():
        for sb, o in _gla_block(qkb_ref, v_ref, tot_row, st_ref, fwd=False, tokens=tokens):
            rs = slice(sb * SUB_TOKENS, (sb + 1) * SUB_TOKENS)
            o = o + of_ref[pl.ds(base + sb * SUB_TOKENS, SUB_TOKENS), :]
            normed = []
            for h in range(GLA_HEADS):
                oh = o[:, h * GLA_DV:(h + 1) * GLA_DV]
                normed.append(oh * lax.rsqrt(jnp.mean(oh * oh, axis=-1, keepdims=True) + LN_EPS))
            on = jnp.concatenate(normed, axis=1) * ng_ref[...]
            o_ref[0, rs, :] = (on * _silu(r_ref[0, rs, :].astype(F32))).astype(BF16)

    if is_ctx:
        @pl.when(j == nblk - 1)
        def _():
            sfin_ref[0, 0] = st_ref[...]


def _gla_ctx(p, qk, tot, norm_g, *, n_batch):
    _, L, _ = p.shape
    tb = SEQ_TOKENS
    st_shape = ((GLA_HEADS // 2) * GLA_DV, 2 * GLA_DK)
    return pl.pallas_call(
        functools.partial(_gla_kernel, tokens=tb, nblk=1, is_ctx=True),
        grid=(n_batch, 2, 1),
        in_specs=[
            pl.BlockSpec((1, tb, QK_COLS), lambda b, ph, j: (n_batch, b, 0)),
            pl.BlockSpec((1, tb, QK_COLS), lambda b, ph, j: (n_batch, b, 1)),
            pl.BlockSpec((1, tb, GLA_WIDTH), lambda b, ph, j: (n_batch, b, 1)),
            pl.BlockSpec((1, tb, GLA_WIDTH), lambda b, ph, j: (n_batch, b, 2)),
            pl.BlockSpec((1, 2 * tb // GLA_CHUNK, 2 * GLA_KW), lambda b, ph, j: (n_batch, b // 2, 0)),
            _resident((1, GLA_WIDTH)),
        ],
        out_specs=[
            pl.BlockSpec((1, tb, GLA_WIDTH), lambda b, ph, j: (0, b, 0)),
            pl.BlockSpec((1, 1) + st_shape, lambda b, ph, j: (b, ph, 0, 0)),
        ],
        out_shape=[
            jax.ShapeDtypeStruct((1, L, GLA_WIDTH), BF16),
            jax.ShapeDtypeStruct((n_batch, 2) + st_shape, F32),
        ],
        scratch_shapes=[
            pltpu.VMEM((tb, GLA_WIDTH), F32),
            pltpu.VMEM(st_shape, F32),
        ],
        compiler_params=_params("arbitrary", "arbitrary", "arbitrary"),
        name="gla_ctx",
    )(qk, qk, p, p, tot, norm_g.reshape(1, GLA_WIDTH))


def _gla_latent(p, qk, tot, s0, norm_g, *, n_batch):
    _, L, _ = p.shape
    tb = GLA_TOKENS
    nblk = L // tb
    st_shape = ((GLA_HEADS // 2) * GLA_DV, 2 * GLA_DK)

    def visited(ph, j):
        return jnp.where(ph == 0, j, nblk - 1 - j)

    def fwd_only(cidx):
        return lambda b, ph, j: (b, jnp.where(ph == 0, j, nblk - 1), cidx)

    def bwd_only(cidx):
        return lambda b, ph, j: (b, jnp.where(ph == 0, nblk - 1, nblk - 1 - j), cidx)

    return pl.pallas_call(
        functools.partial(_gla_kernel, tokens=tb, nblk=nblk, is_ctx=False),
        grid=(n_batch, 2, nblk),
        in_specs=[
            pl.BlockSpec((1, tb, QK_COLS), fwd_only(0)),
            pl.BlockSpec((1, tb, QK_COLS), bwd_only(1)),
            pl.BlockSpec((1, tb, GLA_WIDTH), lambda b, ph, j: (b, visited(ph, j), 1)),
            pl.BlockSpec((1, tb, GLA_WIDTH), bwd_only(2)),
            pl.BlockSpec((1, tb // GLA_CHUNK, 2 * GLA_KW), lambda b, ph, j: (b, visited(ph, j), 0)),
            pl.BlockSpec((1, 1) + st_shape, lambda b, ph, j: (b, ph, 0, 0)),
            _resident((1, GLA_WIDTH)),
        ],
        out_specs=pl.BlockSpec((1, tb, GLA_WIDTH), bwd_only(0)),
        out_shape=jax.ShapeDtypeStruct((n_batch, L, GLA_WIDTH), BF16),
        scratch_shapes=[
            pltpu.VMEM((L, GLA_WIDTH), F32),
            pltpu.VMEM(st_shape, F32),
        ],
        compiler_params=_params("arbitrary", "arbitrary", "arbitrary"),
        name="gla_latent",
    )(qk, qk, p, p, tot, s0, norm_g.reshape(1, GLA_WIDTH))


def _back_kernel(h_ref, u_ref, up_ref, un_ref, glat_ref, gctx_ref, mod_ref, band_ref, pw_ref, ps_ref, wmix_ref,
                 g1_ref, b1_ref, w_in_ref, w_out_ref, g2_ref, b2_ref, o_ref, ue_ref, a_ref, *, n_batch, alpha):
    tm = h_ref.shape[1]
    i = pl.program_id(1)
    is_ctx = pl.program_id(0) == n_batch
    first = is_ctx | (i == 0)
    last = is_ctx | (i == pl.num_programs(1) - 1)
    no_halo = jnp.zeros((HALO, POOL_WIDTH), BF16)
    ue_ref[0:HALO, :] = jnp.where(first, no_halo, up_ref[0])
    ue_ref[HALO:HALO + tm, :] = u_ref[0]
    ue_ref[HALO + tm:, :] = jnp.where(last, no_halo, un_ref[0])

    pos = lax.broadcasted_iota(jnp.int32, (POOL_TOKENS, 1), 0)
    nsub = tm // POOL_TOKENS

    groups = [slice(s * POOL_TOKENS, (s + 1) * POOL_TOKENS) for s in range(nsub)]
    wsums = [[_dot(band_ref[0, g], ue_ref[rs.start:rs.stop + 2 * HALO, g * POOL_GROUP:(g + 1) * POOL_GROUP])
              for g in range(len(POOL_WINDOWS))] for rs in groups]
    pool_ys = []
    for s, rs in enumerate(groups):
        starts = first if s == 0 else is_ctx
        ends = last if s == nsub - 1 else is_ctx
        room_lo = jnp.where(starts, pos, POOL_TOKENS)
        room_hi = jnp.where(ends, POOL_TOKENS - 1 - pos, POOL_TOKENS)
        yg = []
        for g, w in enumerate(POOL_WINDOWS):
            cs = slice(g * POOL_GROUP, (g + 1) * POOL_GROUP)
            lo, hi = w // 2, w - 1 - w // 2
            cnt = (jnp.minimum(room_lo, lo) + jnp.minimum(room_hi, hi) + 1).astype(F32)
            pooled = wsums[s][g] / cnt - ue_ref[HALO + rs.start:HALO + rs.stop, cs].astype(F32)
            yg.append(_dot(pooled.astype(BF16), pw_ref[0, g]))
        pool_ys.append((jnp.concatenate(yg, axis=1) * ps_ref[0]).astype(BF16))
    ys = []
    for rs, pool_y in zip(groups, pool_ys):
        gla = jnp.where(is_ctx, gctx_ref[0, rs, :], glat_ref[0, rs, :])
        ys.append(_dot(pool_y, wmix_ref[0, :POOL_WIDTH, :]) + _dot(gla, wmix_ref[0, POOL_WIDTH:, :]))
    for rs, y in zip(groups, ys):
        z = alpha * h_ref[0, rs, :] + mod_ref[0, 5:6, :] * y
        o_ref[0, rs, :] = _layernorm(z) * g1_ref[0] + b1_ref[0]
    for rs in groups:
        o_ref[0, rs, :] = _ffn_rows(o_ref[0, rs, :], mod_ref, 6, w_in_ref, w_out_ref, a_ref, rs, g2_ref, b2_ref,
                                    alpha)


def _pool_bands():
    t = jnp.arange(POOL_TOKENS)[:, None]
    j = jnp.arange(POOL_TOKENS + 2 * HALO)[None, :] - HALO
    inside = (j >= 0) & (j < POOL_TOKENS)
    bands = []
    for w in POOL_WINDOWS:
        lo, hi = w // 2, w - 1 - w // 2
        bands.append((j >= t - lo) & (j <= t + hi))
    bands = jnp.stack(bands)
    return jnp.stack([bands, bands & inside]).astype(BF16)


def _back(h, p, g_lat, g_ctx, mod, l, bands, pool_w, pool_scale, w_mix_out, ln_g, ln_b, w_in, w_out,
          *, n_batch, alpha, nseg):
    _, L, D = h.shape
    tm = BACK_TOKENS
    nt = L // tm
    hb = tm // HALO
    return pl.pallas_call(
        functools.partial(_back_kernel, n_batch=n_batch, alpha=alpha),
        grid=(nseg, nt),
        in_specs=[
            pl.BlockSpec((1, tm, D), lambda b, i: (b, i, 0)),
            pl.BlockSpec((1, tm, POOL_WIDTH), lambda b, i: (b, i, 0)),
            pl.BlockSpec((1, HALO, POOL_WIDTH), lambda b, i: (b, jnp.maximum(i * hb - 1, 0), 0)),
            pl.BlockSpec((1, HALO, POOL_WIDTH), lambda b, i: (b, jnp.minimum((i + 1) * hb, nt * hb - 1), 0)),
            pl.BlockSpec((1, tm, GLA_WIDTH),
                         lambda b, i: (jnp.minimum(b, n_batch - 1), jnp.where(b == n_batch, nt - 1, i), 0)),
            pl.BlockSpec((1, tm, GLA_WIDTH), lambda b, i: (0, jnp.where(b == n_batch, i, 0), 0)),
            pl.BlockSpec((1, N_MOD, D), lambda b, i: (b, 0, 0)),
            pl.BlockSpec((1,) + bands.shape[1:], lambda b, i: (jnp.where(b == n_batch, 1, 0), 0, 0, 0)),
            _layer_block(pool_w, l), _layer_block(pool_scale, l), _layer_block(w_mix_out, l),
            _layer_block(ln_g, 3 * l + 1), _layer_block(ln_b, 3 * l + 1),
            _layer_block(w_in, l), _layer_block(w_out, l),
            _layer_block(ln_g, 3 * l + 2), _layer_block(ln_b, 3 * l + 2),
        ],
        out_specs=pl.BlockSpec((1, tm, D), lambda b, i: (b, i, 0)),
        out_shape=jax.ShapeDtypeStruct((nseg, L, D), F32),
        scratch_shapes=[pltpu.VMEM((tm + 2 * HALO, POOL_WIDTH), BF16), pltpu.VMEM((tm, D_FF), BF16)],
        compiler_params=_params("arbitrary", "arbitrary"),
        name="back",
    )(h, p, p, p, g_lat, g_ctx, mod, bands, pool_w, pool_scale, w_mix_out, ln_g, ln_b, w_in, w_out, ln_g, ln_b)


def _pos_embed_2d(L):
    rows = L // GRID_W
    quarter = D_MODEL // 4
    omega = 1.0 / (10000.0 ** (jnp.arange(quarter, dtype=F32) / quarter))

    def enc(n):
        a = jnp.arange(n, dtype=F32)[:, None] * omega
        return jnp.concatenate([jnp.sin(a), jnp.cos(a)], axis=-1)

    return jnp.concatenate([jnp.repeat(enc(rows), GRID_W, axis=0), jnp.tile(enc(GRID_W), (rows, 1))], axis=-1)


def kernel(x, c, ctx, c_ctx, w_ada, b_ada, ln_g, ln_b, ffa_w_in, ffa_w_out, mix_w_in, pool_w, pool_scale,
           gate_up_f, gate_bias_f, gate_up_b, gate_bias_b, gla_norm_g, mix_w_out, ffb_w_in, ffb_w_out):
    B, L, D = x.shape
    LC = ctx.shape[1]
    depth = w_ada.shape[0]
    assert D == D_MODEL and LC == SEQ_TOKENS == POOL_TOKENS == FFN_ROWS == CUM_TOKENS and B * LC == L
    assert L % FFN_TOKENS == 0 and L % BACK_TOKENS == 0 and L % GLA_TOKENS == 0
    alpha = (2.0 * depth) ** 0.25
    nseg = B + 1

    rows = -(-nseg // 8) * 8
    cond = jnp.concatenate([c, c_ctx[None, :], jnp.zeros((rows - nseg, D), F32)], axis=0)
    mod = _ada(cond, w_ada, b_ada)[:, :nseg].reshape(depth, nseg, N_MOD, D)

    zeros_gd = jnp.zeros((depth, D, GD_PAD - 2 * GATE_RANK), BF16)
    w_mix_in = jnp.concatenate([mix_w_in.astype(BF16), zeros_gd], axis=-1)
    gup = jnp.zeros((depth, GD_PAD, 2 * GLA_KW), F32)
    gup = gup.at[:, :GATE_RANK, :GLA_KW].set(gate_up_f)
    gup = gup.at[:, GATE_RANK:2 * GATE_RANK, GLA_KW:].set(gate_up_b).astype(BF16)
    gbias = jnp.concatenate([gate_bias_f, gate_bias_b], axis=-1).reshape(depth, 1, 2 * GLA_KW)
    ffa_in, ffa_out = ffa_w_in.astype(BF16), ffa_w_out.astype(BF16)
    ffb_in, ffb_out = ffb_w_in.astype(BF16), ffb_w_out.astype(BF16)
    w_mix_out = mix_w_out.astype(BF16)
    pw = pool_w.astype(BF16)
    ps = pool_scale.reshape(depth, 1, POOL_WIDTH)
    lng = ln_g.reshape(depth * 3, 1, D)
    lnb = ln_b.reshape(depth * 3, 1, D)
    ti = jnp.arange(CUM_TOKENS)
    same_chunk = (ti[:, None] // GLA_CHUNK) == (ti[None, :] // GLA_CHUNK)
    tri_f = (same_chunk & (ti[None, :] <= ti[:, None])).astype(BF16)
    tri_b = (same_chunk & (ti[None, :] >= ti[:, None])).astype(BF16)
    bands = _pool_bands()

    h = (x, _pos_embed_2d(L), ctx.reshape(1, L, D))
    for l in range(depth):
        last = l == depth - 1
        h, p, qk, tot = _front(h, mod[l], l, ffa_in, ffa_out, lng, lnb, w_mix_in, gup, gbias, tri_f, tri_b,
                               n_batch=B, alpha=alpha)
        g_ctx, s_ctx = _gla_ctx(p, qk, tot, gla_norm_g[l], n_batch=B)
        g_lat = _gla_latent(p, qk, tot, s_ctx, gla_norm_g[l], n_batch=B)
        h = _back(h, p, g_lat, g_ctx, mod[l], l, bands, pw, ps, w_mix_out, lng, lnb, ffb_in, ffb_out,
                  n_batch=B, alpha=alpha, nseg=B if last else nseg)
    return h
```
